```python
import math
import jax, jax.numpy as jnp
from jax import lax
import numpy as np

D_MODEL = 4096
BATCH = 4
SEQ = 2048
DEPTH = 2
DEC_BATCH = 128
DEC_SEQ = 1
PAST_LEN = 16384
PAGE_SIZE = 128

N_MIXERS = 2
N_CONF = (DEPTH + 1) // 2
N_GDN = DEPTH // 2
PLE_DIM = 256
D_FF = -(-8 * D_MODEL // (3 * 256)) * 256
CONF_WIDTH = 31
D_CONV = D_MODEL
GDN_HEADS = D_MODEL // 128
GDN_DK = 128
GDN_DV = 128
SHORT_CONV = 4
CHUNK = 64
QKV_DIM = GDN_HEADS * (2 * GDN_DK + GDN_DV)
GDN_PROJ = QKV_DIM + GDN_HEADS * GDN_DV + 2 * GDN_HEADS
RMS_EPS = 1e-6
LN_EPS = 1e-5
L2_EPS = 1e-6

kernel_name = 'hybrid_conformer_gdn_decoder_step'


def rmsnorm(x, g, eps=RMS_EPS):
    xf = x.astype(jnp.float32)
    y = xf * lax.rsqrt(jnp.mean(xf * xf, axis=-1, keepdims=True) + eps)
    return (y * g.astype(jnp.float32)).astype(x.dtype)


def layernorm(x, g, b, eps=LN_EPS):
    xf = x.astype(jnp.float32)
    mu = jnp.mean(xf, axis=-1, keepdims=True)
    xc = xf - mu
    y = xc * lax.rsqrt(jnp.mean(xc * xc, axis=-1, keepdims=True) + eps)
    return (y * g.astype(jnp.float32) + b.astype(jnp.float32)).astype(x.dtype)


def l2norm(x, eps=L2_EPS):
    return x * lax.rsqrt(jnp.sum(x * x, axis=-1, keepdims=True) + eps)


def causal_depthwise_conv(x, buf, w):
    x_ext = jnp.concatenate([buf.astype(x.dtype), x], axis=1)
    y = lax.conv_general_dilated(x_ext, w[:, None, :].astype(x.dtype), (1,), 'VALID',
                                 dimension_numbers=('NWC', 'WIO', 'NWC'),
                                 feature_group_count=x.shape[-1])
    return y, x_ext[:, x_ext.shape[1] - (w.shape[0] - 1):]


def conformer_conv(u, buf, w_pw1, w_dw, b_dw, ln_g, ln_b, w_pw2):
    a, gt = jnp.split(u @ w_pw1, 2, axis=-1)
    glu = a * jax.nn.sigmoid(gt)
    c, new_buf = causal_depthwise_conv(glu, buf, w_dw)
    c = layernorm(c + b_dw, ln_g, ln_b)
    return jax.nn.silu(c) @ w_pw2, new_buf


def gated_delta_rule_chunked(q, k, v, g, beta, S0):
    B_, T, H, _ = q.shape
    DVv = v.shape[-1]
    C = min(CHUNK, T)
    pad = (-T) % C
    if pad:
        padt = lambda a: jnp.pad(a, [(0, 0), (0, pad)] + [(0, 0)] * (a.ndim - 2))
        q, k, v, g, beta = padt(q), padt(k), padt(v), padt(g), padt(beta)
    N = (T + pad) // C

    def to_chunks(a):
        a = a.reshape((B_, N, C) + a.shape[2:])
        return jnp.moveaxis(a, (1, 3), (0, 2))

    qc, kc, vc, gc, bc = to_chunks(q), to_chunks(k), to_chunks(v), to_chunks(g), to_chunks(beta)
    G = jnp.cumsum(gc, axis=-1)
    causal = jnp.tril(jnp.ones((C, C), dtype=bool))
    strict = jnp.tril(jnp.ones((C, C), dtype=bool), -1)
    diff = G[..., :, None] - G[..., None, :]
    decay = jnp.where(causal, jnp.exp(jnp.where(causal, diff, 0.0)), 0.0)
    kb = kc * bc[..., None]
    Bm = jnp.where(strict, jnp.einsum('nbhik,nbhjk->nbhij', kb, kc) * decay, 0.0)
    rhs = jnp.concatenate([vc * bc[..., None], kb * jnp.exp(G)[..., None]], axis=-1)
    sol = lax.linalg.triangular_solve(Bm, rhs, left_side=True, lower=True, unit_diagonal=True)
    value, kcd = sol[..., :DVv], sol[..., DVv:]
    attn = jnp.einsum('nbhik,nbhjk->nbhij', qc, kc) * decay
    q_dec = qc * jnp.exp(G)[..., None]
    k_dec = kc * jnp.exp(G[..., -1:] - G)[..., None]
    a_last = jnp.exp(G[..., -1])

    def step(S, xs):
        value_n, kcd_n, attn_n, qd_n, kd_n, al_n = xs
        u = value_n - jnp.einsum('bhck,bhkv->bhcv', kcd_n, S)
        o = jnp.einsum('bhck,bhkv->bhcv', qd_n, S) + jnp.einsum('bhij,bhjv->bhiv', attn_n, u)
        S = S * al_n[..., None, None] + jnp.einsum('bhck,bhcv->bhkv', kd_n, u)
        return S, o

    S_fin, o = lax.scan(step, S0, (value, kcd, attn, q_dec, k_dec, a_last))
    o = jnp.moveaxis(o, (0, 2), (1, 3)).reshape(B_, N * C, H, DVv)[:, :T]
    return o, S_fin


def gated_deltanet(u, conv_buf, S0, w_in, w_conv, a_log, dt_bias, g_onorm, w_out):
    B_, T, _ = u.shape
    H, DK, DV = GDN_HEADS, GDN_DK, GDN_DV
    proj = u @ w_in
    qkv, z, b_logit, a_logit = jnp.split(proj, [QKV_DIM, QKV_DIM + H * DV, QKV_DIM + H * DV + H], axis=-1)
    qkv_c, new_buf = causal_depthwise_conv(qkv, conv_buf, w_conv)
    qkv_c = jax.nn.silu(qkv_c).astype(jnp.float32)
    q, k, v = jnp.split(qkv_c, [H * DK, 2 * H * DK], axis=-1)
    q = l2norm(q.reshape(B_, T, H, DK)) * (DK ** -0.5)
    k = l2norm(k.reshape(B_, T, H, DK))
    v = v.reshape(B_, T, H, DV)
    beta = jax.nn.sigmoid(b_logit.astype(jnp.float32))
    g = -jnp.exp(a_log.astype(jnp.float32)) * jax.nn.softplus(a_logit.astype(jnp.float32) + dt_bias.astype(jnp.float32))
    o, S_new = gated_delta_rule_chunked(q, k, v, g, beta, S0.astype(jnp.float32))
    o = rmsnorm(o, g_onorm) * jax.nn.silu(z.reshape(B_, T, H, DV).astype(jnp.float32))
    out = o.reshape(B_, T, H * DV).astype(u.dtype) @ w_out
    return out, new_buf, S_new.astype(S0.dtype)


def swiglu(u, w_gate, w_up, w_down):
    return (jax.nn.silu(u @ w_gate) * (u @ w_up)) @ w_down


def trunk(x, p, conf_buf, qkv_buf, gdn_state, prm):
    h = x
    conf_new, qkv_new, gdn_new = [], [], []
    for i in range(DEPTH):
        j = i // N_MIXERS
        u = rmsnorm(h, prm['g_mix'][i])
        if i % N_MIXERS == 0:
            out, nb = conformer_conv(u, conf_buf[j], prm['conf_w_pw1'][j], prm['conf_w_dw'][j], prm['conf_b_dw'][j],
                                     prm['conf_ln_g'][j], prm['conf_ln_b'][j], prm['conf_w_pw2'][j])
            conf_new.append(nb)
        else:
            out, nb, ns = gated_deltanet(u, qkv_buf[j], gdn_state[j], prm['gdn_w_in'][j], prm['gdn_w_conv'][j],
                                         prm['gdn_a_log'][j], prm['gdn_dt_bias'][j], prm['gdn_g_onorm'][j], prm['gdn_w_out'][j])
            qkv_new.append(nb)
            gdn_new.append(ns)
        h = h + out
        h = h + swiglu(rmsnorm(h, prm['g_ffn'][i]), prm['ffn_w_gate'][i], prm['ffn_w_up'][i], prm['ffn_w_down'][i])
        gate = jax.nn.sigmoid(rmsnorm(h, prm['g_ple'][i]) @ prm['ple_w_gate'][i])
        h = h + gate * (p[i].astype(h.dtype) @ prm['ple_w_proj'][i])
    return rmsnorm(h, prm['g_final']), jnp.stack(conf_new), jnp.stack(qkv_new), jnp.stack(gdn_new)


def _normal(key, shape, scale):
    return jax.random.normal(key, shape, jnp.float32) * scale


def setup_inputs(seed: int = 0) -> dict:
    key = jax.random.key(seed)
    k = jax.random.split(key, 32)
    d, f, h = D_MODEL, D_FF, GDN_HEADS
    gain = lambda kk, shape: 1.0 + _normal(kk, shape, 0.02)
    x_prompt = _normal(k[0], (BATCH, SEQ, d), 1.0)
    x_sample = _normal(k[1], (DEC_BATCH, DEC_SEQ, d), 1.0)
    p_prompt = _normal(k[2], (DEPTH, BATCH, SEQ, PLE_DIM), 1.0)
    p_sample = _normal(k[3], (DEPTH, DEC_BATCH, DEC_SEQ, PLE_DIM), 1.0)
    state_conv_conformer = _normal(k[4], (N_CONF, DEC_BATCH, CONF_WIDTH - 1, D_CONV), 0.5)
    state_conv_qkv = _normal(k[5], (N_GDN, DEC_BATCH, SHORT_CONV - 1, QKV_DIM), 1.0)
    state_gdn = _normal(k[6], (N_GDN, DEC_BATCH, h, GDN_DK, GDN_DV), GDN_DK ** -0.5)
    g_mix = gain(k[7], (DEPTH, d))
    g_ffn = gain(k[8], (DEPTH, d))
    g_ple = gain(k[9], (DEPTH, d))
    g_final = gain(k[10], (d,))
    conf_w_pw1 = _normal(k[11], (N_CONF, d, 2 * D_CONV), d ** -0.5)
    conf_w_dw = _normal(k[12], (N_CONF, CONF_WIDTH, D_CONV), CONF_WIDTH ** -0.5)
    conf_b_dw = _normal(k[13], (N_CONF, D_CONV), 0.02)
    conf_ln_g = gain(k[14], (N_CONF, D_CONV))
    conf_ln_b = _normal(k[15], (N_CONF, D_CONV), 0.02)
    conf_w_pw2 = _normal(k[16], (N_CONF, D_CONV, d), D_CONV ** -0.5)
    gdn_w_in = _normal(k[17], (N_GDN, d, GDN_PROJ), d ** -0.5)
    gdn_w_conv = _normal(k[18], (N_GDN, SHORT_CONV, QKV_DIM), SHORT_CONV ** -0.5)
    gdn_a_log = jnp.log(jax.random.uniform(k[19], (N_GDN, h), jnp.float32, 1.0, 16.0))
    dt = jnp.exp(jax.random.uniform(k[20], (N_GDN, h), jnp.float32, math.log(1e-3), math.log(1e-1)))
    gdn_dt_bias = dt + jnp.log(-jnp.expm1(-dt))
    gdn_g_onorm = gain(k[21], (N_GDN, GDN_DV))
    gdn_w_out = _normal(k[22], (N_GDN, h * GDN_DV, d), (h * GDN_DV) ** -0.5)
    ffn_w_gate = _normal(k[23], (DEPTH, d, f), d ** -0.5)
    ffn_w_up = _normal(k[24], (DEPTH, d, f), d ** -0.5)
    ffn_w_down = _normal(k[25], (DEPTH, f, d), f ** -0.5)
    ple_w_gate = _normal(k[26], (DEPTH, d, d), d ** -0.5)
    ple_w_proj = _normal(k[27], (DEPTH, PLE_DIM, d), PLE_DIM ** -0.5)
    return {'x_prompt': x_prompt, 'x_sample': x_sample, 'p_prompt': p_prompt, 'p_sample': p_sample,
            'state_conv_conformer': state_conv_conformer, 'state_conv_qkv': state_conv_qkv, 'state_gdn': state_gdn,
            'g_mix': g_mix, 'g_ffn': g_ffn, 'g_ple': g_ple, 'g_final': g_final,
            'conf_w_pw1': conf_w_pw1, 'conf_w_dw': conf_w_dw, 'conf_b_dw': conf_b_dw,
            'conf_ln_g': conf_ln_g, 'conf_ln_b': conf_ln_b, 'conf_w_pw2': conf_w_pw2,
            'gdn_w_in': gdn_w_in, 'gdn_w_conv': gdn_w_conv, 'gdn_a_log': gdn_a_log, 'gdn_dt_bias': gdn_dt_bias,
            'gdn_g_onorm': gdn_g_onorm, 'gdn_w_out': gdn_w_out,
            'ffn_w_gate': ffn_w_gate, 'ffn_w_up': ffn_w_up, 'ffn_w_down': ffn_w_down,
            'ple_w_gate': ple_w_gate, 'ple_w_proj': ple_w_proj}


def reference(x_prompt, x_sample, p_prompt, p_sample, state_conv_conformer, state_conv_qkv, state_gdn,
              g_mix, g_ffn, g_ple, g_final, conf_w_pw1, conf_w_dw, conf_b_dw, conf_ln_g, conf_ln_b, conf_w_pw2,
              gdn_w_in, gdn_w_conv, gdn_a_log, gdn_dt_bias, gdn_g_onorm, gdn_w_out,
              ffn_w_gate, ffn_w_up, ffn_w_down, ple_w_gate, ple_w_proj):
    prm = dict(g_mix=g_mix, g_ffn=g_ffn, g_ple=g_ple, g_final=g_final,
               conf_w_pw1=conf_w_pw1, conf_w_dw=conf_w_dw, conf_b_dw=conf_b_dw,
               conf_ln_g=conf_ln_g, conf_ln_b=conf_ln_b, conf_w_pw2=conf_w_pw2,
               gdn_w_in=gdn_w_in, gdn_w_conv=gdn_w_conv, gdn_a_log=gdn_a_log, gdn_dt_bias=gdn_dt_bias,
               gdn_g_onorm=gdn_g_onorm, gdn_w_out=gdn_w_out,
               ffn_w_gate=ffn_w_gate, ffn_w_up=ffn_w_up, ffn_w_down=ffn_w_down,
               ple_w_gate=ple_w_gate, ple_w_proj=ple_w_proj)
    bp = x_prompt.shape[0]
    conf0 = jnp.zeros((N_CONF, bp, CONF_WIDTH - 1, D_CONV), x_prompt.dtype)
    qkv0 = jnp.zeros((N_GDN, bp, SHORT_CONV - 1, QKV_DIM), x_prompt.dtype)
    gdn0 = jnp.zeros((N_GDN, bp, GDN_HEADS, GDN_DK, GDN_DV), state_gdn.dtype)
    y_prompt, conf_p, qkv_p, gdn_p = trunk(x_prompt, p_prompt, conf0, qkv0, gdn0, prm)
    y_sample, conf_s, qkv_s, gdn_s = trunk(x_sample, p_sample, state_conv_conformer, state_conv_qkv, state_gdn, prm)
    return (y_prompt, y_sample, conf_p, qkv_p, gdn_p, conf_s, qkv_s, gdn_s)
```

```python
import functools

import jax
import jax.numpy as jnp
from jax import lax
from jax.experimental import pallas as pl
from jax.experimental.pallas import tpu as pltpu

F32 = jnp.float32
BF16 = jnp.bfloat16

RMS_EPS = 1e-6
LN_EPS = 1e-5
L2_EPS = 1e-6
CHUNK = 64
HEAD_DIM = 128
LANES = 128
SUBLANES = 8
V7X_VMEM_LIMIT_CAP = 60000 * 1024
VMEM_INTERNAL_SCRATCH = 8 * 2**20


def _pick_tile(n, cap, mult):
    best = None
    for d in range(mult, min(n, cap) + 1, mult):
        if n % d == 0:
            best = d
    return n if best is None else best


def _params(sem, block_bytes, scratch_bytes=0):
    need = 2 * block_bytes + scratch_bytes + VMEM_INTERNAL_SCRATCH
    return pltpu.CompilerParams(dimension_semantics=sem,
                                vmem_limit_bytes=int(min(max(need, 32 * 2**20), V7X_VMEM_LIMIT_CAP)))


def _nbytes(shape, dtype):
    n = 1
    for s in shape:
        n *= s
    return n * jnp.dtype(dtype).itemsize


def _bdot(a, b):
    return jnp.dot(a.astype(BF16), b.astype(BF16), preferred_element_type=F32)


def _silu(x):
    return x * jax.nn.sigmoid(x)


def _rms_body(x_ref, g_ref, o_ref):
    x = x_ref[...]
    y = x * lax.rsqrt(jnp.mean(x * x, axis=-1, keepdims=True) + RMS_EPS)
    o_ref[...] = (y * g_ref[...]).astype(o_ref.dtype)


def _rmsnorm(x, g, out_dtype, row0=0, rows=None):
    M, D = x.shape
    rows = M if rows is None else rows
    tm = _pick_tile(rows, 320, 16)
    assert row0 % tm == 0
    blk = _nbytes((tm, D), F32) + _nbytes((tm, D), out_dtype)
    return pl.pallas_call(
        _rms_body,
        grid=(rows // tm,),
        in_specs=[pl.BlockSpec((tm, D), lambda i: (i + row0 // tm, 0)),
                  pl.BlockSpec((1, D), lambda i: (0, 0))],
        out_specs=pl.BlockSpec((tm, D), lambda i: (i, 0)),
        out_shape=jax.ShapeDtypeStruct((rows, D), out_dtype),
        compiler_params=_params(("parallel",), blk),
        name="rmsnorm",
    )(x, g.reshape(1, D))


def _mm_glu_body(x_ref, wa_ref, wg_ref, o_ref):
    x = x_ref[...]
    a = jnp.dot(x, wa_ref[...], preferred_element_type=F32)
    g = jnp.dot(x, wg_ref[...], preferred_element_type=F32)
    o_ref[...] = (a * jax.nn.sigmoid(g)).astype(o_ref.dtype)


def _mm_swiglu_body(x_ref, wg_ref, wu_ref, o_ref):
    x = x_ref[...]
    g = jnp.dot(x, wg_ref[...], preferred_element_type=F32)
    u = jnp.dot(x, wu_ref[...], preferred_element_type=F32)
    o_ref[...] = (_silu(g) * u).astype(o_ref.dtype)


def _mm_res_body(x_ref, w_ref, r_ref, o_ref):
    o_ref[...] = r_ref[...] + jnp.dot(x_ref[...], w_ref[...], preferred_element_type=F32)


def _mm_plain_body(x_ref, w_ref, o_ref):
    o_ref[...] = jnp.dot(x_ref[...], w_ref[...], preferred_element_type=F32).astype(o_ref.dtype)


def _mm_ple_body(x_ref, p_ref, wg_ref, wp_ref, r_ref, o_ref):
    g = jnp.dot(x_ref[...], wg_ref[...], preferred_element_type=F32)
    pp = jnp.dot(p_ref[...], wp_ref[...], preferred_element_type=F32)
    o_ref[...] = r_ref[...] + jax.nn.sigmoid(g) * pp


def _matmul(body, name, M, N, tm, tn, operands, out_dtype):
    blk = sum(_nbytes(bs, a.dtype) for a, bs, _ in operands) + _nbytes((tm, tn), out_dtype)
    return pl.pallas_call(
        body,
        grid=(M // tm, N // tn),
        in_specs=[pl.BlockSpec(bs, im) for _, bs, im in operands],
        out_specs=pl.BlockSpec((tm, tn), lambda i, j: (i, j)),
        out_shape=jax.ShapeDtypeStruct((M, N), out_dtype),
        compiler_params=_params(("parallel", "arbitrary"), blk),
        name=name,
    )(*[a for a, _, _ in operands])


def _x_spec(x, tm):
    return (x, (tm, x.shape[1]), lambda i, j: (i, 0))


def _w_spec(w, tn, col_block0=0):
    return (w, (w.shape[0], tn), lambda i, j: (0, j + col_block0))


def _r_spec(r, tm, tn):
    return (r, (tm, tn), lambda i, j: (i, j))


def _mm_tiles(M, K, N, n_w):
    tn = _pick_tile(N, 512 if K * n_w <= 4096 else 256, LANES)
    tm_cap = 1040 if K <= 4096 else 640
    return _pick_tile(M, tm_cap, 16), tn


def _conv_blocks(load_blk, w_row, n_taps, n_out, width):
    qmax = (n_taps - 1) // SUBLANES
    rows = lax.broadcasted_iota(jnp.int32, (SUBLANES, width), 0)
    x = {k: load_blk(k) for k in range(-(qmax + 1), n_out)}
    acc = [None] * n_out
    for r in range(min(SUBLANES, n_taps)):
        if r == 0:
            y = x
        else:
            rot = {k: pltpu.roll(v, r, 0) for k, v in x.items()}
            y = {k: jnp.where(rows >= r, rot[k], rot[k - 1]) for k in range(-qmax, n_out)}
        for q in range(qmax + 1):
            d = SUBLANES * q + r
            if d >= n_taps:
                continue
            w = w_row(d)
            for j in range(n_out):
                t = y[j - q] * w
                acc[j] = t if acc[j] is None else acc[j] + t
    return acc


CONF_HALO = 32
CONV_STRIP = 8
CONV_ROWS_CAP = 256
SCAN_ROWS_CAP = 512


def _conf_conv_body(x_ref, halo_ref, w_ref, b_ref, lg_ref, lb_ref, o_ref, xe_ref, y_ref, *, tb, ncb, n_taps):
    i = pl.program_id(1)
    first = i == 0
    for cb in range(ncb):
        sl = slice(cb * LANES, (cb + 1) * LANES)
        xe_ref[cb, CONF_HALO:CONF_HALO + tb, :] = x_ref[:, sl]
        xe_ref[cb, 0:CONF_HALO, :] = jnp.where(first, 0.0, halo_ref[:, sl])

    strip_rows = SUBLANES * CONV_STRIP
    n_strips = tb // strip_rows

    def col_loop(cb, carry):
        def strip_loop(s, carry2):
            r0 = pl.multiple_of(s * strip_rows, strip_rows)
            acc = _conv_blocks(
                lambda k: xe_ref[cb, pl.ds(r0 + (CONF_HALO + SUBLANES * k), SUBLANES), :],
                lambda d: w_ref[cb, pl.ds(n_taps - 1 - d, 1), :],
                n_taps, CONV_STRIP, LANES)
            bias = b_ref[cb]
            for j in range(CONV_STRIP):
                y_ref[cb, pl.ds(r0 + SUBLANES * j, SUBLANES), :] = acc[j] + bias
            return carry2
        return lax.fori_loop(0, n_strips, strip_loop, carry)

    lax.fori_loop(0, ncb, col_loop, 0)

    ln_rows = 32
    inv_c = 1.0 / (ncb * LANES)

    def ln_loop(s, carry):
        r0 = pl.multiple_of(s * ln_rows, ln_rows)
        y = y_ref[:, pl.ds(r0, ln_rows), :]
        mu = jnp.sum(jnp.sum(y, axis=0), axis=-1, keepdims=True) * inv_c
        yc = y - mu[None]
        var = jnp.sum(jnp.sum(yc * yc, axis=0), axis=-1, keepdims=True) * inv_c
        rstd = lax.rsqrt(var + LN_EPS)
        for cb in range(ncb):
            t = yc[cb] * rstd * lg_ref[cb] + lb_ref[cb]
            o_ref[pl.ds(r0, ln_rows), cb * LANES:(cb + 1) * LANES] = _silu(t).astype(o_ref.dtype)
        return carry

    lax.fori_loop(0, tb // ln_rows, ln_loop, 0)


def _col_major(v, pad_rows=None):
    R, C = v.shape
    out = v.reshape(R, C // LANES, LANES).transpose(1, 0, 2)
    if pad_rows is not None and pad_rows > R:
        out = jnp.pad(out, ((0, 0), (0, pad_rows - R), (0, 0)))
    return out


def _conf_conv_prompt(glu, Bp, T, w_dw, b_dw, ln_g, ln_b):
    Dc = glu.shape[1]
    n_taps = w_dw.shape[0]
    assert n_taps - 1 <= CONF_HALO and T >= CONF_HALO
    tb = _pick_tile(T, CONV_ROWS_CAP, SUBLANES * CONV_STRIP)
    ncb = Dc // LANES
    nt = T // tb
    body = functools.partial(_conf_conv_body, tb=tb, ncb=ncb, n_taps=n_taps)
    hb = tb // CONF_HALO
    blk = (_nbytes((tb, Dc), F32) + _nbytes((CONF_HALO, Dc), F32) + _nbytes((tb, Dc), BF16)
           + _nbytes((ncb, 32 + 3, LANES), F32))
    scratch = _nbytes((ncb, tb + CONF_HALO, LANES), F32) + _nbytes((ncb, tb, LANES), F32)
    return pl.pallas_call(
        body,
        grid=(Bp, nt),
        in_specs=[
            pl.BlockSpec((tb, Dc), lambda b, i: (b * nt + i, 0)),
            pl.BlockSpec((CONF_HALO, Dc), lambda b, i: (jnp.maximum((b * nt + i) * hb - 1, 0), 0)),
            pl.BlockSpec((ncb, 32, LANES), lambda b, i: (0, 0, 0)),
            pl.BlockSpec((ncb, 1, LANES), lambda b, i: (0, 0, 0)),
            pl.BlockSpec((ncb, 1, LANES), lambda b, i: (0, 0, 0)),
            pl.BlockSpec((ncb, 1, LANES), lambda b, i: (0, 0, 0)),
        ],
        out_specs=pl.BlockSpec((tb, Dc), lambda b, i: (b * nt + i, 0)),
        out_shape=jax.ShapeDtypeStruct((Bp * T, Dc), BF16),
        scratch_shapes=[pltpu.VMEM((ncb, tb + CONF_HALO, LANES), F32), pltpu.VMEM((ncb, tb, LANES), F32)],
        compiler_params=_params(("parallel", "parallel"), blk, scratch),
        name="conf_conv_prompt",
    )(glu, glu, _col_major(w_dw, 32), _col_major(b_dw[None]), _col_major(ln_g[None]), _col_major(ln_b[None]))


def _conf_conv_sample_body(st_ref, x_ref, w_ref, wl_ref, b_ref, lg_ref, lb_ref, o_ref, acc_ref):
    w = pl.program_id(0)

    @pl.when(w == 0)
    def _():
        acc_ref[...] = x_ref[...] * wl_ref[...]

    acc_ref[...] += st_ref[...] * w_ref[0]

    @pl.when(w == pl.num_programs(0) - 1)
    def _():
        y = acc_ref[...] + b_ref[...]
        mu = jnp.mean(y, axis=-1, keepdims=True)
        yc = y - mu
        rstd = lax.rsqrt(jnp.mean(yc * yc, axis=-1, keepdims=True) + LN_EPS)
        o_ref[...] = _silu(yc * rstd * lg_ref[...] + lb_ref[...]).astype(o_ref.dtype)


def _conf_conv_sample(glu, Mp, Bs, state, w_dw, b_dw, ln_g, ln_b):
    Dc = glu.shape[1]
    n_prev = w_dw.shape[0] - 1
    assert Mp % Bs == 0
    blk = 3 * _nbytes((Bs, Dc), F32) + 5 * _nbytes((1, Dc), F32)
    row = lambda w: (0, 0)
    return pl.pallas_call(
        _conf_conv_sample_body,
        grid=(n_prev,),
        in_specs=[
            pl.BlockSpec((Bs, Dc), lambda w: (0, w)),
            pl.BlockSpec((Bs, Dc), lambda w: (Mp // Bs, 0)),
            pl.BlockSpec((1, 1, Dc), lambda w: (w, 0, 0)),
            pl.BlockSpec((1, Dc), row), pl.BlockSpec((1, Dc), row), pl.BlockSpec((1, Dc), row), pl.BlockSpec((1, Dc), row),
        ],
        out_specs=pl.BlockSpec((Bs, Dc), lambda w: (0, 0)),
        out_shape=jax.ShapeDtypeStruct((Bs, Dc), BF16),
        scratch_shapes=[pltpu.VMEM((Bs, Dc), F32)],
        compiler_params=_params(("arbitrary",), blk, _nbytes((Bs, Dc), F32)),
        name="conf_conv_sample",
    )(state.reshape(Bs, n_prev * Dc), glu, w_dw[:n_prev, None, :], w_dw[n_prev:], b_dw[None], ln_g[None], ln_b[None])


def _qkv_finish(y, kind):
    y = _silu(y)
    ss = jnp.sum(y * y, axis=-1, keepdims=True)
    qscale = jnp.where(kind == 0, HEAD_DIM ** -0.5, 1.0).astype(F32)
    fac = jnp.where(kind == 2, 1.0, lax.rsqrt(ss + L2_EPS) * qscale)
    return y * fac


QKV_HALO = 8


def _qkv_prep_body(x_ref, halo_ref, w_ref, o_ref, xe_ref, *, tb, cw, n_taps, groups_per_kind):
    i = pl.program_id(1)
    kind = pl.program_id(2) // groups_per_kind
    xe_ref[QKV_HALO:QKV_HALO + tb, :] = x_ref[...]
    xe_ref[0:QKV_HALO, :] = jnp.where(i == 0, 0.0, halo_ref[...])
    strip_rows = SUBLANES * CONV_STRIP

    def strip_loop(s, carry):
        r0 = pl.multiple_of(s * strip_rows, strip_rows)
        for cb in range(cw // LANES):
            sl = slice(cb * LANES, (cb + 1) * LANES)
            acc = _conv_blocks(
                lambda k: xe_ref[pl.ds(r0 + (QKV_HALO + SUBLANES * k), SUBLANES), sl],
                lambda d: w_ref[pl.ds(n_taps - 1 - d, 1), sl],
                n_taps, CONV_STRIP, LANES)
            for j in range(CONV_STRIP):
                o_ref[pl.ds(r0 + SUBLANES * j, SUBLANES), sl] = _qkv_finish(acc[j], kind)
        return carry

    lax.fori_loop(0, tb // strip_rows, strip_loop, 0)


def _qkv_prep_prompt(proj, Bp, T, qkv_dim, w_conv):
    n_taps = w_conv.shape[0]
    assert n_taps - 1 <= QKV_HALO
    tb = _pick_tile(T, CONV_ROWS_CAP, SUBLANES * CONV_STRIP)
    cw = _pick_tile(qkv_dim // 3, 1024, LANES)
    nt = T // tb
    ncg = qkv_dim // cw
    body = functools.partial(_qkv_prep_body, tb=tb, cw=cw, n_taps=n_taps, groups_per_kind=ncg // 3)
    hb = tb // QKV_HALO
    blk = 2 * _nbytes((tb, cw), F32) + 2 * _nbytes((QKV_HALO, cw), F32)
    return pl.pallas_call(
        body,
        grid=(Bp, nt, ncg),
        in_specs=[
            pl.BlockSpec((tb, cw), lambda b, i, c: (b * nt + i, c)),
            pl.BlockSpec((QKV_HALO, cw), lambda b, i, c: (jnp.maximum((b * nt + i) * hb - 1, 0), c)),
            pl.BlockSpec((n_taps, cw), lambda b, i, c: (0, c)),
        ],
        out_specs=pl.BlockSpec((tb, cw), lambda b, i, c: (b * nt + i, c)),
        out_shape=jax.ShapeDtypeStruct((Bp * T, qkv_dim), F32),
        scratch_shapes=[pltpu.VMEM((tb + QKV_HALO, cw), F32)],
        compiler_params=_params(("parallel", "parallel", "parallel"), blk, _nbytes((tb + QKV_HALO, cw), F32)),
        name="qkv_prep_prompt",
    )(proj, proj, w_conv)


def _qkv_prep_sample_body(*refs, n_prev, cw, groups_per_kind):
    st_refs, x_ref, w_ref, o_ref = refs[:n_prev], refs[n_prev], refs[n_prev + 1], refs[n_prev + 2]
    kind = pl.program_id(0) // groups_per_kind
    y = x_ref[...] * w_ref[n_prev:n_prev + 1, :]
    for w in range(n_prev):
        y = y + st_refs[w][...] * w_ref[w:w + 1, :]
    for cb in range(cw // LANES):
        sl = slice(cb * LANES, (cb + 1) * LANES)
        o_ref[:, sl] = _qkv_finish(y[:, sl], kind)


def _qkv_prep_sample(proj, Mp, Bs, qkv_dim, state, w_conv):
    n_prev = w_conv.shape[0] - 1
    cw = _pick_tile(qkv_dim // 3, 1024, LANES)
    ncg = qkv_dim // cw
    st2 = state.reshape(Bs, n_prev * qkv_dim)
    body = functools.partial(_qkv_prep_sample_body, n_prev=n_prev, cw=cw, groups_per_kind=ncg // 3)
    blk = (n_prev + 2) * _nbytes((Bs, cw), F32) + _nbytes((n_prev + 1, cw), F32)
    st_specs = [pl.BlockSpec((Bs, cw), functools.partial(lambda c, w: (0, w * ncg + c), w=w)) for w in range(n_prev)]
    return pl.pallas_call(
        body,
        grid=(ncg,),
        in_specs=st_specs + [pl.BlockSpec((Bs, cw), lambda c: (Mp // Bs, c)),
                             pl.BlockSpec((n_prev + 1, cw), lambda c: (0, c))],
        out_specs=pl.BlockSpec((Bs, cw), lambda c: (0, c)),
        out_shape=jax.ShapeDtypeStruct((Bs, qkv_dim), F32),
        compiler_params=_params(("parallel",), blk),
        name="qkv_prep_sample",
    )(*([st2] * n_prev), proj, w_conv)


def _gates_body(l_ref, alog_ref, dtb_ref, beta_ref, g_ref, *, n_heads, rows, chunk):
    lg = l_ref[...]
    beta_ref[...] = jax.nn.sigmoid(lg[:, :n_heads])
    g = -jnp.exp(alog_ref[...]) * jax.nn.softplus(lg[:, n_heads:] + dtb_ref[...])
    if chunk == 1:
        g_ref[...] = jnp.exp(g)
    else:
        tril = (lax.broadcasted_iota(jnp.int32, (chunk, chunk), 0)
                >= lax.broadcasted_iota(jnp.int32, (chunk, chunk), 1)).astype(F32)
        for c in range(rows // chunk):
            g_ref[c * chunk:(c + 1) * chunk, :] = jnp.dot(
                tril, g[c * chunk:(c + 1) * chunk, :], preferred_element_type=F32, precision=lax.Precision.HIGHEST)


def _gates(logits, a_log, dt_bias, row0, rows, chunk):
    H = logits.shape[1] // 2
    tr = _pick_tile(rows, 512, max(chunk, SUBLANES))
    assert row0 % tr == 0
    body = functools.partial(_gates_body, n_heads=H, rows=tr, chunk=chunk)
    return pl.pallas_call(
        body,
        grid=(rows // tr,),
        in_specs=[pl.BlockSpec((tr, 2 * H), lambda i: (i + row0 // tr, 0)),
                  pl.BlockSpec((1, H), lambda i: (0, 0)), pl.BlockSpec((1, H), lambda i: (0, 0))],
        out_specs=[pl.BlockSpec((tr, H), lambda i: (i, 0)), pl.BlockSpec((tr, H), lambda i: (i, 0))],
        out_shape=[jax.ShapeDtypeStruct((rows, H), F32), jax.ShapeDtypeStruct((rows, H), F32)],
        compiler_params=_params(("parallel",), 4 * _nbytes((tr, LANES), F32)),
        name="gdn_gates",
    )(logits, a_log.reshape(1, H), dt_bias.reshape(1, H))


def _gdn_scan_body(q_ref, k_ref, v_ref, z_ref, beta_ref, g_ref, grow_ref, gon_ref, o_ref, s_ref, *, hb, tt):
    C = CHUNK

    @pl.when(pl.program_id(2) == 0)
    def _():
        s_ref[...] = jnp.zeros_like(s_ref)

    ri = lax.broadcasted_iota(jnp.int32, (C, C), 0)
    ci = lax.broadcasted_iota(jnp.int32, (C, C), 1)
    causal = ri >= ci
    strict = ri > ci
    eye = (ri == ci).astype(F32)
    blk = lambda s: (ri // s) == (ci // s)
    gon = gon_ref[...]

    def chunk_loop(c, carry):
        r0 = pl.multiple_of(c * C, C)
        for hl in range(hb):
            hs = slice(hl * HEAD_DIM, (hl + 1) * HEAD_DIM)
            q = q_ref[pl.ds(r0, C), hs]
            k = k_ref[pl.ds(r0, C), hs]
            v = v_ref[pl.ds(r0, C), hs]
            bcol = beta_ref[0, pl.ds(r0, C), hl:hl + 1]
            gcol = g_ref[0, pl.ds(r0, C), hl:hl + 1]
            grow = grow_ref[c, hl:hl + 1, :]
            glast = gcol[C - 1:C, :]
            decay = jnp.where(causal, jnp.exp(jnp.where(causal, gcol - grow, 0.0)), 0.0)
            eg = jnp.exp(gcol)
            kb = k * bcol
            k16 = k.astype(BF16)
            nt = (((1,), (1,)), ((), ()))
            kk = lax.dot_general(kb.astype(BF16), k16, nt, preferred_element_type=F32)
            qk = lax.dot_general(q.astype(BF16), k16, nt, preferred_element_type=F32)
            bm = jnp.where(strict, kk * decay, 0.0)
            attn = qk * decay
            x0 = jnp.where(blk(SUBLANES), -bm, 0.0)
            tinv = eye + x0
            p = _bdot(x0, x0)
            tinv = tinv + _bdot(tinv, p)
            p = _bdot(p, p)
            tinv = tinv + _bdot(tinv, p)
            s = SUBLANES
            while s < C:
                off = jnp.where(jnp.logical_and(blk(2 * s), jnp.logical_not(blk(s))), bm, 0.0)
                tinv = tinv - _bdot(_bdot(tinv, off), tinv)
                s *= 2
            value = _bdot(tinv, v * bcol)
            kcd = _bdot(tinv, kb * eg)
            st = s_ref[0, hl]
            u = value - _bdot(kcd, st)
            o = _bdot(q * eg, st) + _bdot(attn, u)
            kd = k * jnp.exp(glast - gcol)
            tn = (((0,), (0,)), ((), ()))
            s_ref[0, hl] = st * jnp.exp(glast) + lax.dot_general(
                kd.astype(BF16), u.astype(BF16), tn, preferred_element_type=F32)
            on = o * lax.rsqrt(jnp.mean(o * o, axis=-1, keepdims=True) + RMS_EPS) * gon
            zz = z_ref[pl.ds(r0, C), hs]
            o_ref[pl.ds(r0, C), hs] = (on * _silu(zz)).astype(o_ref.dtype)
        return carry

    lax.fori_loop(0, tt // C, chunk_loop, 0)


def _gdn_scan_prompt(qkv, proj, z_col0, beta, gcum, g_onorm, Bp, T, H):
    Mp = Bp * T
    hb = _pick_tile(H, 8, 8) if H % 8 == 0 else H
    tt = _pick_tile(T, SCAN_ROWS_CAP, CHUNK)
    assert T % CHUNK == 0 and (hb % 8 == 0 or hb == H)
    nhg, ntt = H // hb, T // tt
    cw = hb * HEAD_DIM
    assert z_col0 % cw == 0
    beta_g = beta.reshape(Mp, nhg, hb).transpose(1, 0, 2)
    gcol_g = gcum.reshape(Mp, nhg, hb).transpose(1, 0, 2)
    grow_g = gcum.reshape(Mp // CHUNK, CHUNK, H).transpose(0, 2, 1)
    body = functools.partial(_gdn_scan_body, hb=hb, tt=tt)
    row = lambda b, h, t: b * ntt + t
    blk = (4 * _nbytes((tt, cw), F32) + 2 * _nbytes((tt, LANES), F32) + _nbytes((tt // CHUNK, hb, LANES), F32)
           + _nbytes((tt, cw), BF16) + _nbytes((hb, HEAD_DIM, HEAD_DIM), F32))
    return pl.pallas_call(
        body,
        grid=(Bp, nhg, ntt),
        in_specs=[
            pl.BlockSpec((tt, cw), lambda b, h, t: (row(b, h, t), h)),
            pl.BlockSpec((tt, cw), lambda b, h, t: (row(b, h, t), nhg + h)),
            pl.BlockSpec((tt, cw), lambda b, h, t: (row(b, h, t), 2 * nhg + h)),
            pl.BlockSpec((tt, cw), lambda b, h, t: (row(b, h, t), z_col0 // cw + h)),
            pl.BlockSpec((1, tt, hb), lambda b, h, t: (h, row(b, h, t), 0)),
            pl.BlockSpec((1, tt, hb), lambda b, h, t: (h, row(b, h, t), 0)),
            pl.BlockSpec((tt // CHUNK, hb, CHUNK), lambda b, h, t: (row(b, h, t), h, 0)),
            pl.BlockSpec((1, HEAD_DIM), lambda b, h, t: (0, 0)),
        ],
        out_specs=[
            pl.BlockSpec((tt, cw), lambda b, h, t: (row(b, h, t), h)),
            pl.BlockSpec((1, hb, HEAD_DIM, HEAD_DIM), lambda b, h, t: (b, h, 0, 0)),
        ],
        out_shape=[jax.ShapeDtypeStruct((Mp, H * HEAD_DIM), BF16),
                   jax.ShapeDtypeStruct((Bp, H, HEAD_DIM, HEAD_DIM), F32)],
        compiler_params=_params(("parallel", "parallel", "arbitrary"), blk),
        name="gdn_scan_prompt",
    )(qkv, qkv, qkv, proj, beta_g, gcol_g, grow_g, g_onorm.reshape(1, HEAD_DIM))


def _gdn_step_body(a_ref, beta_ref, s_ref, kt_ref, qt_ref, v_ref, z_ref, gon_ref, so_ref, o_ref, orow_ref, *, n_heads):
    b = pl.program_id(0)
    for h in range(n_heads):
        st = s_ref[0, h]
        a = a_ref[b, h]
        bt = beta_ref[b, h]
        kc = kt_ref[0, :, h:h + 1]
        qc = qt_ref[0, :, h:h + 1]
        sk = jnp.sum(st * kc, axis=0, keepdims=True)
        w = v_ref[0, h:h + 1, :] - a * sk
        sn = a * st + (bt * kc) * w
        so_ref[0, h] = sn
        orow_ref[h:h + 1, :] = jnp.sum(sn * qc, axis=0, keepdims=True)
    o = orow_ref[...]
    on = o * lax.rsqrt(jnp.mean(o * o, axis=-1, keepdims=True) + RMS_EPS) * gon_ref[...]
    o_ref[0] = (on * _silu(z_ref[0])).astype(o_ref.dtype)


def _gdn_step_sample(qkv_s, z_s, beta_s, a_s, state, g_onorm):
    Bs, H = beta_s.shape
    hd = H * HEAD_DIM
    q3 = qkv_s[:, :hd].reshape(Bs, H, HEAD_DIM).transpose(0, 2, 1)
    k3 = qkv_s[:, hd:2 * hd].reshape(Bs, H, HEAD_DIM).transpose(0, 2, 1)
    v3 = qkv_s[:, 2 * hd:].reshape(Bs, H, HEAD_DIM)
    z3 = z_s.reshape(Bs, H, HEAD_DIM)
    smem = pl.BlockSpec(memory_space=pltpu.SMEM)
    body = functools.partial(_gdn_step_body, n_heads=H)
    blk = 2 * _nbytes((H, HEAD_DIM, HEAD_DIM), F32) + 6 * _nbytes((HEAD_DIM, LANES), F32)
    so, o = pl.pallas_call(
        body,
        grid=(Bs,),
        in_specs=[smem, smem,
                  pl.BlockSpec((1, H, HEAD_DIM, HEAD_DIM), lambda b: (b, 0, 0, 0)),
                  pl.BlockSpec((1, HEAD_DIM, H), lambda b: (b, 0, 0)),
                  pl.BlockSpec((1, HEAD_DIM, H), lambda b: (b, 0, 0)),
                  pl.BlockSpec((1, H, HEAD_DIM), lambda b: (b, 0, 0)),
                  pl.BlockSpec((1, H, HEAD_DIM), lambda b: (b, 0, 0)),
                  pl.BlockSpec((1, HEAD_DIM), lambda b: (0, 0))],
        out_specs=[pl.BlockSpec((1, H, HEAD_DIM, HEAD_DIM), lambda b: (b, 0, 0, 0)),
                   pl.BlockSpec((1, H, HEAD_DIM), lambda b: (b, 0, 0))],
        out_shape=[jax.ShapeDtypeStruct((Bs, H, HEAD_DIM, HEAD_DIM), F32),
                   jax.ShapeDtypeStruct((Bs, H, HEAD_DIM), BF16)],
        scratch_shapes=[pltpu.VMEM((H, HEAD_DIM), F32)],
        compiler_params=_params(("arbitrary",), blk),
        name="gdn_step_sample",
    )(a_s, beta_s, state, k3, q3, v3, z3, g_onorm.reshape(1, HEAD_DIM))
    return o.reshape(Bs, hd), so


def _conformer_layer(h, u, Mp, Bp, T, Bs, state, w_pw1, w_dw, b_dw, ln_g, ln_b, w_pw2):
    M, D = h.shape
    Dc = w_pw1.shape[1] // 2
    tm, tn = _mm_tiles(M, D, Dc, 2)
    glu = _matmul(_mm_glu_body, "mm_glu", M, Dc, tm, tn,
                  [_x_spec(u, tm), _w_spec(w_pw1, tn), _w_spec(w_pw1, tn, Dc // tn)], F32)
    c_p = _conf_conv_prompt(glu, Bp, T, w_dw, b_dw, ln_g, ln_b)
    c_s = _conf_conv_sample(glu, Mp, Bs, state, w_dw, b_dw, ln_g, ln_b)
    c = jnp.concatenate([c_p, c_s], axis=0)
    tm, tn = _mm_tiles(M, Dc, D, 1)
    h = _matmul(_mm_res_body, "mm_pw2", M, D, tm, tn, [_x_spec(c, tm), _w_spec(w_pw2, tn), _r_spec(h, tm, tn)], F32)
    n_prev = w_dw.shape[0] - 1
    new_p = glu[:Mp].reshape(Bp, T, Dc)[:, T - n_prev:]
    new_s = jnp.concatenate([state[:, 1:], glu[Mp:, None, :]], axis=1)
    return h, new_p, new_s


def _gdn_layer(h, u, Mp, Bp, T, Bs, conv_state, gdn_state, w_in, w_conv, a_log, dt_bias, g_onorm, w_out):
    M, D = h.shape
    H = a_log.shape[0]
    hd = H * HEAD_DIM
    qkv_dim = 3 * hd
    n_main = qkv_dim + hd
    tm, tn = _mm_tiles(M, D, n_main, 1)
    proj = _matmul(_mm_plain_body, "mm_gdn_in", M, n_main, tm, tn, [_x_spec(u, tm), _w_spec(w_in, tn)], F32)
    w_gate = w_in[:, n_main:]
    logits = _matmul(_mm_plain_body, "mm_gdn_gates", M, 2 * H, tm, 2 * H, [_x_spec(u, tm), _w_spec(w_gate, 2 * H)], F32)

    beta_p, gcum_p = _gates(logits, a_log, dt_bias, 0, Mp, CHUNK)
    beta_s, g_s = _gates(logits, a_log, dt_bias, Mp, Bs, 1)

    qkv_p = _qkv_prep_prompt(proj, Bp, T, qkv_dim, w_conv)
    qkv_s = _qkv_prep_sample(proj, Mp, Bs, qkv_dim, conv_state, w_conv)

    o_p, s_p = _gdn_scan_prompt(qkv_p, proj, qkv_dim, beta_p, gcum_p, g_onorm, Bp, T, H)
    o_s, s_s = _gdn_step_sample(qkv_s, proj[Mp:, qkv_dim:], beta_s, g_s, gdn_state, g_onorm)
    o = jnp.concatenate([o_p, o_s], axis=0)
    tm, tn = _mm_tiles(M, hd, D, 1)
    h = _matmul(_mm_res_body, "mm_gdn_out", M, D, tm, tn, [_x_spec(o, tm), _w_spec(w_out, tn), _r_spec(h, tm, tn)], F32)
    n_prev = w_conv.shape[0] - 1
    new_conv_p = proj[:Mp, :qkv_dim].reshape(Bp, T, qkv_dim)[:, T - n_prev:]
    new_conv_s = jnp.concatenate([conv_state[:, 1:], proj[Mp:, None, :qkv_dim]], axis=1)
    return h, new_conv_p, new_conv_s, s_p, s_s


def _ffn_ple(h, p, g_ffn, w_gate, w_up, w_down, g_ple, w_pg, w_pp):
    M, D = h.shape
    F = w_gate.shape[1]
    u = _rmsnorm(h, g_ffn, BF16)
    tm, tn = _mm_tiles(M, D, F, 2)
    act = _matmul(_mm_swiglu_body, "mm_ffn_up", M, F, tm, tn, [_x_spec(u, tm), _w_spec(w_gate, tn), _w_spec(w_up, tn)], BF16)
    tm, tn = _mm_tiles(M, F, D, 1)
    h = _matmul(_mm_res_body, "mm_ffn_down", M, D, tm, tn, [_x_spec(act, tm), _w_spec(w_down, tn), _r_spec(h, tm, tn)], F32)
    u = _rmsnorm(h, g_ple, BF16)
    tm, tn = _mm_tiles(M, D + p.shape[1], D, 2)
    h = _matmul(_mm_ple_body, "mm_ple", M, D, tm, tn,
                [_x_spec(u, tm), _x_spec(p, tm), _w_spec(w_pg, tn), _w_spec(w_pp, tn), _r_spec(h, tm, tn)], F32)
    return h


def kernel(x_prompt, x_sample, p_prompt, p_sample, state_conv_conformer, state_conv_qkv, state_gdn, g_mix, g_ffn, g_ple, g_final, conf_w_pw1, conf_w_dw, conf_b_dw, conf_ln_g, conf_ln_b, conf_w_pw2, gdn_w_in, gdn_w_conv, gdn_a_log, gdn_dt_bias, gdn_g_onorm, gdn_w_out, ffn_w_gate, ffn_w_up, ffn_w_down, ple_w_gate, ple_w_proj):
    Bp, T, D = x_prompt.shape
    Bs, Ts, _ = x_sample.shape
    assert Ts == 1
    depth = g_mix.shape[0]
    Mp = Bp * T
    M = Mp + Bs
    h = jnp.concatenate([x_prompt.reshape(Mp, D), x_sample.reshape(Bs, D)], axis=0)
    p_all = jnp.concatenate([p_prompt.reshape(depth, Mp, -1), p_sample.reshape(depth, Bs, -1)], axis=1).astype(BF16)
    bf = lambda w: w.astype(BF16)

    conf_p, conf_s, qkv_p, qkv_s, gdn_p, gdn_s = [], [], [], [], [], []
    for i in range(depth):
        j = i // 2
        u = _rmsnorm(h, g_mix[i], BF16)
        if i % 2 == 0:
            h, np_, ns_ = _conformer_layer(h, u, Mp, Bp, T, Bs, state_conv_conformer[j], bf(conf_w_pw1[j]), conf_w_dw[j],
                                           conf_b_dw[j], conf_ln_g[j], conf_ln_b[j], bf(conf_w_pw2[j]))
            conf_p.append(np_)
            conf_s.append(ns_)
        else:
            h, cp_, cs_, sp_, ss_ = _gdn_layer(h, u, Mp, Bp, T, Bs, state_conv_qkv[j], state_gdn[j], bf(gdn_w_in[j]),
                                               gdn_w_conv[j], gdn_a_log[j], gdn_dt_bias[j], gdn_g_onorm[j], bf(gdn_w_out[j]))
            qkv_p.append(cp_)
            qkv_s.append(cs_)
            gdn_p.append(sp_)
            gdn_s.append(ss_)
        h = _ffn_ple(h, p_all[i], g_ffn[i], bf(ffn_w_gate[i]), bf(ffn_w_up[i]), bf(ffn_w_down[i]),
                     g_ple[i], bf(ple_w_gate[i]), bf(ple_w_proj[i]))

    y_p = _rmsnorm(h, g_final, F32, 0, Mp).reshape(Bp, T, D)
    y_s = _rmsnorm(h, g_final, F32, Mp, Bs).reshape(Bs, 1, D)
    return (y_p, y_s, jnp.stack(conf_p), jnp.stack(qkv_p), jnp.stack(gdn_p),
            jnp.stack(conf_s), jnp.stack(qkv_s), jnp.stack(gdn_s))
```

```python
import functools

import jax
import jax.numpy as jnp
from jax import lax
from jax.experimental import pallas as pl
from jax.experimental.pallas import tpu as pltpu

F32 = jnp.float32
BF16 = jnp.bfloat16

RMS_EPS = 1e-6
LN_EPS = 1e-5
L2_EPS = 1e-6
CHUNK = 256
HEAD_DIM = 128
LANES = 128
SUBLANES = 8
V7X_VMEM_LIMIT_CAP = 60000 * 1024
VMEM_INTERNAL_SCRATCH = 8 * 2**20


def _pick_tile(n, cap, mult):
    best = None
    for d in range(mult, min(n, cap) + 1, mult):
        if n % d == 0:
            best = d
    return n if best is None else best


def _params(sem, block_bytes, scratch_bytes=0):
    need = 2 * block_bytes + scratch_bytes + VMEM_INTERNAL_SCRATCH
    return pltpu.CompilerParams(dimension_semantics=sem,
                                vmem_limit_bytes=int(min(max(need, 32 * 2**20), V7X_VMEM_LIMIT_CAP)))


def _nbytes(shape, dtype):
    n = 1
    for s in shape:
        n *= 1 if s is None else s
    return n * jnp.dtype(dtype).itemsize


def _bdot(a, b):
    return jnp.dot(a.astype(BF16), b.astype(BF16), preferred_element_type=F32)


def _silu(x):
    return x * jax.nn.sigmoid(x)


def _rms_body(x_ref, g_ref, o_ref):
    x = x_ref[...]
    y = x * lax.rsqrt(jnp.mean(x * x, axis=-1, keepdims=True) + RMS_EPS)
    o_ref[...] = (y * g_ref[...]).astype(o_ref.dtype)


def _rmsnorm(x, g, out_dtype, row0=0, rows=None):
    M, D = x.shape
    rows = M if rows is None else rows
    tm = _pick_tile(rows, 320, 16)
    assert row0 % tm == 0
    blk = _nbytes((tm, D), F32) + _nbytes((tm, D), out_dtype)
    return pl.pallas_call(
        _rms_body,
        grid=(rows // tm,),
        in_specs=[pl.BlockSpec((tm, D), lambda i: (i + row0 // tm, 0)),
                  pl.BlockSpec((1, D), lambda i: (0, 0))],
        out_specs=pl.BlockSpec((tm, D), lambda i: (i, 0)),
        out_shape=jax.ShapeDtypeStruct((rows, D), out_dtype),
        compiler_params=_params(("parallel",), blk),
        name="rmsnorm",
    )(x, g.reshape(1, D))


def _wdot(x_ref, w_ref):
    return jnp.dot(x_ref[...], w_ref[...].astype(BF16), preferred_element_type=F32)


def _mm_glu_body(x_ref, wa_ref, wg_ref, o_ref):
    o_ref[...] = (_wdot(x_ref, wa_ref) * jax.nn.sigmoid(_wdot(x_ref, wg_ref))).astype(o_ref.dtype)


def _mm_swiglu_body(x_ref, wg_ref, wu_ref, o_ref):
    o_ref[...] = (_silu(_wdot(x_ref, wg_ref)) * _wdot(x_ref, wu_ref)).astype(o_ref.dtype)


def _mm_res_body(x_ref, w_ref, r_ref, o_ref):
    o_ref[...] = r_ref[...] + _wdot(x_ref, w_ref)


def _mm_plain_body(x_ref, w_ref, o_ref):
    o_ref[...] = _wdot(x_ref, w_ref).astype(o_ref.dtype)


def _mm_ple_body(x_ref, p_ref, wg_ref, wp_ref, r_ref, o_ref):
    o_ref[...] = r_ref[...] + jax.nn.sigmoid(_wdot(x_ref, wg_ref)) * _wdot(p_ref, wp_ref)


def _matmul(body, name, M, N, tm, tn, operands, out_dtype):
    blk = sum(_nbytes(bs, a.dtype) for a, bs, _ in operands) + _nbytes((tm, tn), out_dtype)
    return pl.pallas_call(
        body,
        grid=(M // tm, N // tn),
        in_specs=[pl.BlockSpec(bs, im) for _, bs, im in operands],
        out_specs=pl.BlockSpec((tm, tn), lambda i, j: (i, j)),
        out_shape=jax.ShapeDtypeStruct((M, N), out_dtype),
        compiler_params=_params(("parallel", "arbitrary"), blk),
        name=name,
    )(*[a for a, _, _ in operands])


def _x_spec(x, tm):
    return (x, (tm, x.shape[1]), lambda i, j: (i, 0))


def _w_spec(w, layer, tn, col_block0=0):
    return (w, (None, w.shape[1], tn), lambda i, j: (layer, 0, j + col_block0))


def _r_spec(r, tm, tn):
    return (r, (tm, tn), lambda i, j: (i, j))


def _mm_tiles(M, K, N, w_bytes_per_col):
    tn = _pick_tile(N, 512 if 512 * w_bytes_per_col <= 8 * 2**20 else 256, LANES)
    tm_cap = 1040 if K <= 4096 else 640
    return _pick_tile(M, tm_cap, 16), tn


def _conv_blocks(load_blk, w_row, n_taps, n_out, width):
    qmax = (n_taps - 1) // SUBLANES
    rows = lax.broadcasted_iota(jnp.int32, (SUBLANES, width), 0)
    x = {k: load_blk(k) for k in range(-(qmax + 1), n_out)}
    acc = [None] * n_out
    for r in range(min(SUBLANES, n_taps)):
        if r == 0:
            y = x
        else:
            rot = {k: pltpu.roll(v, r, 0) for k, v in x.items()}
            y = {k: jnp.where(rows >= r, rot[k], rot[k - 1]) for k in range(-qmax, n_out)}
        for q in range(qmax + 1):
            d = SUBLANES * q + r
            if d >= n_taps:
                continue
            w = w_row(d)
            for j in range(n_out):
                t = y[j - q] * w
                acc[j] = t if acc[j] is None else acc[j] + t
    return acc


CONF_HALO = 32
CONV_STRIP = 8
CONV_ROWS_CAP = 256
SCAN_ROWS_CAP = 512


def _conf_conv_body(x_ref, halo_ref, cs_ref, w_ref, b_ref, lg_ref, lb_ref, o_ref, xe_ref, y_ref, *,
                    tb, ncb, n_taps, tiles_per_seq, n_tiles):
    step = pl.program_id(0)

    @pl.when(step == n_tiles)
    def _():
        o_ref[0:cs_ref.shape[0], :] = cs_ref[...].astype(o_ref.dtype)

    @pl.when(step < n_tiles)
    def _():
        _conf_conv_tile(x_ref, halo_ref, w_ref, b_ref, lg_ref, lb_ref, o_ref, xe_ref, y_ref,
                        first=step % tiles_per_seq == 0, tb=tb, ncb=ncb, n_taps=n_taps)


def _conf_conv_tile(x_ref, halo_ref, w_ref, b_ref, lg_ref, lb_ref, o_ref, xe_ref, y_ref, *, first, tb, ncb, n_taps):
    for cb in range(ncb):
        sl = slice(cb * LANES, (cb + 1) * LANES)
        xe_ref[cb, CONF_HALO:CONF_HALO + tb, :] = x_ref[:, sl]
        xe_ref[cb, 0:CONF_HALO, :] = jnp.where(first, 0.0, halo_ref[:, sl])

    strip_rows = SUBLANES * CONV_STRIP
    n_strips = tb // strip_rows

    def col_loop(cb, carry):
        def strip_loop(s, carry2):
            r0 = pl.multiple_of(s * strip_rows, strip_rows)
            acc = _conv_blocks(
                lambda k: xe_ref[cb, pl.ds(r0 + (CONF_HALO + SUBLANES * k), SUBLANES), :],
                lambda d: w_ref[cb, pl.ds(n_taps - 1 - d, 1), :],
                n_taps, CONV_STRIP, LANES)
            bias = b_ref[cb]
            for j in range(CONV_STRIP):
                y_ref[cb, pl.ds(r0 + SUBLANES * j, SUBLANES), :] = acc[j] + bias
            return carry2
        return lax.fori_loop(0, n_strips, strip_loop, carry)

    lax.fori_loop(0, ncb, col_loop, 0)

    ln_rows = 32
    inv_c = 1.0 / (ncb * LANES)

    def ln_loop(s, carry):
        r0 = pl.multiple_of(s * ln_rows, ln_rows)
        y = y_ref[:, pl.ds(r0, ln_rows), :]
        mu = jnp.sum(jnp.sum(y, axis=0), axis=-1, keepdims=True) * inv_c
        yc = y - mu[None]
        var = jnp.sum(jnp.sum(yc * yc, axis=0), axis=-1, keepdims=True) * inv_c
        rstd = lax.rsqrt(var + LN_EPS)
        for cb in range(ncb):
            t = yc[cb] * rstd * lg_ref[cb] + lb_ref[cb]
            o_ref[pl.ds(r0, ln_rows), cb * LANES:(cb + 1) * LANES] = _silu(t).astype(o_ref.dtype)
        return carry

    lax.fori_loop(0, tb // ln_rows, ln_loop, 0)


def _col_major(v, pad_rows=None):
    R, C = v.shape
    out = v.reshape(R, C // LANES, LANES).transpose(1, 0, 2)
    if pad_rows is not None and pad_rows > R:
        out = jnp.pad(out, ((0, 0), (0, pad_rows - R), (0, 0)))
    return out


def _conf_conv_prompt(glu, Bp, T, c_sample, w_dw, b_dw, ln_g, ln_b):
    Dc = glu.shape[1]
    Bs = c_sample.shape[0]
    n_taps = w_dw.shape[0]
    assert n_taps - 1 <= CONF_HALO and T >= CONF_HALO
    tb = _pick_tile(T, CONV_ROWS_CAP, SUBLANES * CONV_STRIP)
    assert Bs <= tb
    ncb = Dc // LANES
    n_tiles = Bp * T // tb
    body = functools.partial(_conf_conv_body, tb=tb, ncb=ncb, n_taps=n_taps, tiles_per_seq=T // tb, n_tiles=n_tiles)
    hb = tb // CONF_HALO
    blk = (_nbytes((tb, Dc), F32) + _nbytes((CONF_HALO, Dc), F32) + _nbytes((tb, Dc), BF16) + _nbytes((Bs, Dc), F32)
           + _nbytes((ncb, 32 + 3, LANES), F32))
    scratch = _nbytes((ncb, tb + CONF_HALO, LANES), F32) + _nbytes((ncb, tb, LANES), F32)
    tile = lambda s: jnp.minimum(s, n_tiles - 1)
    const3 = lambda s: (0, 0, 0)
    return pl.pallas_call(
        body,
        grid=(n_tiles + 1,),
        in_specs=[
            pl.BlockSpec((tb, Dc), lambda s: (tile(s), 0)),
            pl.BlockSpec((CONF_HALO, Dc), lambda s: (jnp.maximum(tile(s) * hb - 1, 0), 0)),
            pl.BlockSpec((Bs, Dc), lambda s: (0, 0)),
            pl.BlockSpec((ncb, 32, LANES), const3),
            pl.BlockSpec((ncb, 1, LANES), const3),
            pl.BlockSpec((ncb, 1, LANES), const3),
            pl.BlockSpec((ncb, 1, LANES), const3),
        ],
        out_specs=pl.BlockSpec((tb, Dc), lambda s: (s, 0)),
        out_shape=jax.ShapeDtypeStruct((Bp * T + Bs, Dc), BF16),
        scratch_shapes=[pltpu.VMEM((ncb, tb + CONF_HALO, LANES), F32), pltpu.VMEM((ncb, tb, LANES), F32)],
        compiler_params=_params(("arbitrary",), blk, scratch),
        name="conf_conv_prompt",
    )(glu, glu, c_sample, _col_major(w_dw, 32), _col_major(b_dw[None]), _col_major(ln_g[None]), _col_major(ln_b[None]))


def _conf_conv_sample_body(st_ref, x_ref, w_ref, wl_ref, b_ref, lg_ref, lb_ref, o_ref, so_ref, y_ref, *, sb, n_prev):
    for b in range(sb):
        xb = x_ref[b:b + 1, :]
        y_ref[b:b + 1, :] = jnp.sum(st_ref[b] * w_ref[...], axis=0, keepdims=True) + xb * wl_ref[...]
        so_ref[b, 0:n_prev - 1, :] = st_ref[b, 1:n_prev, :]
        so_ref[b, n_prev - 1:n_prev, :] = xb
    y = y_ref[...] + b_ref[...]
    mu = jnp.mean(y, axis=-1, keepdims=True)
    yc = y - mu
    rstd = lax.rsqrt(jnp.mean(yc * yc, axis=-1, keepdims=True) + LN_EPS)
    o_ref[...] = _silu(yc * rstd * lg_ref[...] + lb_ref[...])


def _conf_conv_sample(glu, Mp, Bs, state, w_dw, b_dw, ln_g, ln_b):
    Dc = glu.shape[1]
    n_prev = w_dw.shape[0] - 1
    sb = SUBLANES
    assert Mp % sb == 0 and Bs % sb == 0
    blk = 2 * _nbytes((sb, 32, Dc), F32) + 2 * _nbytes((sb, Dc), F32) + _nbytes((32 + 4, Dc), F32)
    row = lambda i: (0, 0)
    body = functools.partial(_conf_conv_sample_body, sb=sb, n_prev=n_prev)
    return pl.pallas_call(
        body,
        grid=(Bs // sb,),
        in_specs=[
            pl.BlockSpec((sb, n_prev, Dc), lambda i: (i, 0, 0)),
            pl.BlockSpec((sb, Dc), lambda i: (Mp // sb + i, 0)),
            pl.BlockSpec((n_prev, Dc), row),
            pl.BlockSpec((1, Dc), row), pl.BlockSpec((1, Dc), row), pl.BlockSpec((1, Dc), row), pl.BlockSpec((1, Dc), row),
        ],
        out_specs=[pl.BlockSpec((sb, Dc), lambda i: (i, 0)), pl.BlockSpec((sb, n_prev, Dc), lambda i: (i, 0, 0))],
        out_shape=[jax.ShapeDtypeStruct((Bs, Dc), F32), jax.ShapeDtypeStruct((Bs, n_prev, Dc), F32)],
        scratch_shapes=[pltpu.VMEM((sb, Dc), F32)],
        compiler_params=_params(("parallel",), blk, _nbytes((sb, Dc), F32)),
        name="conf_conv_sample",
    )(state, glu, w_dw[:n_prev], w_dw[n_prev:], b_dw[None], ln_g[None], ln_b[None])


def _qkv_finish(y, kind):
    y = _silu(y)
    ss = jnp.sum(y * y, axis=-1, keepdims=True)
    qscale = jnp.where(kind == 0, HEAD_DIM ** -0.5, 1.0).astype(F32)
    fac = jnp.where(kind == 2, 1.0, lax.rsqrt(ss + L2_EPS) * qscale)
    return y * fac


QKV_HALO = 8


def _qkv_prep_body(x_ref, halo_ref, w_ref, o_ref, xe_ref, *, tb, cw, n_taps, groups_per_kind):
    i = pl.program_id(1)
    kind = pl.program_id(2) // groups_per_kind
    xe_ref[QKV_HALO:QKV_HALO + tb, :] = x_ref[...]
    xe_ref[0:QKV_HALO, :] = jnp.where(i == 0, 0.0, halo_ref[...])
    strip_rows = SUBLANES * CONV_STRIP

    def strip_loop(s, carry):
        r0 = pl.multiple_of(s * strip_rows, strip_rows)
        for cb in range(cw // LANES):
            sl = slice(cb * LANES, (cb + 1) * LANES)
            acc = _conv_blocks(
                lambda k: xe_ref[pl.ds(r0 + (QKV_HALO + SUBLANES * k), SUBLANES), sl],
                lambda d: w_ref[pl.ds(n_taps - 1 - d, 1), sl],
                n_taps, CONV_STRIP, LANES)
            for j in range(CONV_STRIP):
                o_ref[pl.ds(r0 + SUBLANES * j, SUBLANES), sl] = _qkv_finish(acc[j], kind)
        return carry

    lax.fori_loop(0, tb // strip_rows, strip_loop, 0)


def _qkv_prep_prompt(proj, Bp, T, qkv_dim, w_conv):
    n_taps = w_conv.shape[0]
    assert n_taps - 1 <= QKV_HALO
    tb = _pick_tile(T, CONV_ROWS_CAP, SUBLANES * CONV_STRIP)
    cw = _pick_tile(qkv_dim // 3, 1024, LANES)
    nt = T // tb
    ncg = qkv_dim // cw
    body = functools.partial(_qkv_prep_body, tb=tb, cw=cw, n_taps=n_taps, groups_per_kind=ncg // 3)
    hb = tb // QKV_HALO
    blk = 2 * _nbytes((tb, cw), F32) + 2 * _nbytes((QKV_HALO, cw), F32)
    return pl.pallas_call(
        body,
        grid=(Bp, nt, ncg),
        in_specs=[
            pl.BlockSpec((tb, cw), lambda b, i, c: (b * nt + i, c)),
            pl.BlockSpec((QKV_HALO, cw), lambda b, i, c: (jnp.maximum((b * nt + i) * hb - 1, 0), c)),
            pl.BlockSpec((n_taps, cw), lambda b, i, c: (0, c)),
        ],
        out_specs=pl.BlockSpec((tb, cw), lambda b, i, c: (b * nt + i, c)),
        out_shape=jax.ShapeDtypeStruct((Bp * T, qkv_dim), F32),
        scratch_shapes=[pltpu.VMEM((tb + QKV_HALO, cw), F32)],
        compiler_params=_params(("parallel", "parallel", "parallel"), blk, _nbytes((tb + QKV_HALO, cw), F32)),
        name="qkv_prep_prompt",
    )(proj, proj, w_conv)


def _qkv_prep_sample_body(st_ref, x_ref, w_ref, o_ref, so_ref, *, n_prev, cw, groups_per_kind):
    kind = pl.program_id(0) // groups_per_kind
    x = x_ref[...]
    y = x * w_ref[n_prev:n_prev + 1, :]
    for w in range(n_prev):
        y = y + st_ref[:, w, :] * w_ref[w:w + 1, :]
    for w in range(1, n_prev):
        so_ref[:, w - 1, :] = st_ref[:, w, :]
    so_ref[:, n_prev - 1, :] = x
    for cb in range(cw // LANES):
        sl = slice(cb * LANES, (cb + 1) * LANES)
        o_ref[:, sl] = _qkv_finish(y[:, sl], kind)


def _qkv_prep_sample(proj, Mp, Bs, qkv_dim, state, w_conv):
    n_prev = w_conv.shape[0] - 1
    cw = _pick_tile(qkv_dim // 3, 1024, LANES)
    ncg = qkv_dim // cw
    assert Mp % Bs == 0
    body = functools.partial(_qkv_prep_sample_body, n_prev=n_prev, cw=cw, groups_per_kind=ncg // 3)
    blk = 2 * _nbytes((Bs, SUBLANES, cw), F32) + 2 * _nbytes((Bs, cw), F32) + _nbytes((SUBLANES, cw), F32)
    return pl.pallas_call(
        body,
        grid=(ncg,),
        in_specs=[pl.BlockSpec((Bs, n_prev, cw), lambda c: (0, 0, c)),
                  pl.BlockSpec((Bs, cw), lambda c: (Mp // Bs, c)),
                  pl.BlockSpec((n_prev + 1, cw), lambda c: (0, c))],
        out_specs=[pl.BlockSpec((Bs, cw), lambda c: (0, c)), pl.BlockSpec((Bs, n_prev, cw), lambda c: (0, 0, c))],
        out_shape=[jax.ShapeDtypeStruct((Bs, qkv_dim), F32), jax.ShapeDtypeStruct((Bs, n_prev, qkv_dim), F32)],
        compiler_params=_params(("parallel",), blk),
        name="qkv_prep_sample",
    )(state, proj, w_conv)


def _gates_body(l_ref, alog_ref, dtb_ref, beta_ref, g_ref, *, n_heads, rows, chunk):
    lg = l_ref[...]
    beta_ref[...] = jax.nn.sigmoid(lg[:, :n_heads])
    g = -jnp.exp(alog_ref[...]) * jax.nn.softplus(lg[:, n_heads:] + dtb_ref[...])
    if chunk == 1:
        g_ref[...] = jnp.exp(g)
    else:
        tril = (lax.broadcasted_iota(jnp.int32, (chunk, chunk), 0)
                >= lax.broadcasted_iota(jnp.int32, (chunk, chunk), 1)).astype(F32)
        for c in range(rows // chunk):
            g_ref[c * chunk:(c + 1) * chunk, :] = jnp.dot(
                tril, g[c * chunk:(c + 1) * chunk, :], preferred_element_type=F32, precision=lax.Precision.HIGHEST)


def _gates(logits, a_log, dt_bias, row0, rows, chunk):
    H = logits.shape[1] // 2
    tr = _pick_tile(rows, 512, max(chunk, SUBLANES))
    assert row0 % tr == 0
    body = functools.partial(_gates_body, n_heads=H, rows=tr, chunk=chunk)
    return pl.pallas_call(
        body,
        grid=(rows // tr,),
        in_specs=[pl.BlockSpec((tr, 2 * H), lambda i: (i + row0 // tr, 0)),
                  pl.BlockSpec((1, H), lambda i: (0, 0)), pl.BlockSpec((1, H), lambda i: (0, 0))],
        out_specs=[pl.BlockSpec((tr, H), lambda i: (i, 0)), pl.BlockSpec((tr, H), lambda i: (i, 0))],
        out_shape=[jax.ShapeDtypeStruct((rows, H), F32), jax.ShapeDtypeStruct((rows, H), F32)],
        compiler_params=_params(("parallel",), 4 * _nbytes((tr, LANES), F32)),
        name="gdn_gates",
    )(logits, a_log.reshape(1, H), dt_bias.reshape(1, H))


def _gdn_scan_body(q_ref, k_ref, v_ref, z_ref, beta_ref, g_ref, grow_ref, gon_ref, os_ref, o_ref, s_ref, *,
                   hb, chunks_per_seq, n_chunks):
    step = pl.program_id(1)

    @pl.when(step == n_chunks)
    def _():
        o_ref[0:os_ref.shape[0], :] = os_ref[...]

    @pl.when(step < n_chunks)
    def _():
        @pl.when(step % chunks_per_seq == 0)
        def _():
            s_ref[...] = jnp.zeros_like(s_ref)

        _gdn_chunk(q_ref, k_ref, v_ref, z_ref, beta_ref, g_ref, grow_ref, gon_ref, o_ref, s_ref, hb=hb)


def _gdn_chunk(q_ref, k_ref, v_ref, z_ref, beta_ref, g_ref, grow_ref, gon_ref, o_ref, s_ref, *, hb):
    C = CHUNK

    ri = lax.broadcasted_iota(jnp.int32, (C, C), 0)
    ci = lax.broadcasted_iota(jnp.int32, (C, C), 1)
    causal = ri >= ci
    strict = ri > ci
    eye = (ri == ci).astype(F32)
    blk = lambda s: (ri // s) == (ci // s)
    gon = gon_ref[...]

    heads = range(hb)
    hs = [slice(h * HEAD_DIM, (h + 1) * HEAD_DIM) for h in heads]
    nt = (((1,), (1,)), ((), ()))
    tn = (((0,), (0,)), ((), ()))
    b16 = lambda xs: [x.astype(BF16) for x in xs]
    mm = lambda xs, ys: [jnp.dot(x, y, preferred_element_type=F32) for x, y in zip(xs, ys)]

    q = [q_ref[:, hs[h]] for h in heads]
    k = [k_ref[:, hs[h]] for h in heads]
    bcol = [beta_ref[0, :, h:h + 1] for h in heads]
    gcol = [g_ref[0, :, h:h + 1] for h in heads]
    grow = [grow_ref[0, h:h + 1, :] for h in heads]
    decay = [jnp.where(causal, jnp.exp(jnp.where(causal, gcol[h] - grow[h], 0.0)), 0.0) for h in heads]
    eg = [jnp.exp(gcol[h]) for h in heads]
    kb = [k[h] * bcol[h] for h in heads]
    k16 = b16(k)
    kk = [lax.dot_general(x, y, nt, preferred_element_type=F32) for x, y in zip(b16(kb), k16)]
    qk = [lax.dot_general(x, y, nt, preferred_element_type=F32) for x, y in zip(b16(q), k16)]
    bm = [jnp.where(strict, kk[h] * decay[h], 0.0) for h in heads]
    attn16 = b16([qk[h] * decay[h] for h in heads])

    in8 = blk(SUBLANES)
    x0 = [jnp.where(in8, -bm[h], 0.0) for h in heads]
    tinv = [eye + x0[h] for h in heads]
    x16 = b16(x0)
    p = mm(x16, x16)
    p16 = b16(p)
    tinv = [t + d for t, d in zip(tinv, mm(b16(tinv), p16))]
    p16 = b16(mm(p16, p16))
    tinv = [t + d for t, d in zip(tinv, mm(b16(tinv), p16))]
    s = SUBLANES
    while s < C:
        m = jnp.logical_and(blk(2 * s), jnp.logical_not(blk(s)))
        off16 = b16([jnp.where(m, bm[h], 0.0) for h in heads])
        t16 = b16(tinv)
        tinv = [t - d for t, d in zip(tinv, mm(b16(mm(t16, off16)), t16))]
        s *= 2
    t16 = b16(tinv)

    v = [v_ref[:, hs[h]] for h in heads]
    value = mm(t16, b16([v[h] * bcol[h] for h in heads]))
    kcd16 = b16(mm(t16, b16([kb[h] * eg[h] for h in heads])))
    st = [s_ref[0, h] for h in heads]
    st16 = b16(st)
    u16 = b16([value[h] - d for h, d in zip(heads, mm(kcd16, st16))])
    o = [a + c for a, c in zip(mm(b16([q[h] * eg[h] for h in heads]), st16), mm(attn16, u16))]
    glast = [gcol[h][C - 1:C, :] for h in heads]
    kd16 = b16([k[h] * jnp.exp(glast[h] - gcol[h]) for h in heads])
    ku = [lax.dot_general(x, y, tn, preferred_element_type=F32) for x, y in zip(kd16, u16)]
    for h in heads:
        s_ref[0, h] = st[h] * jnp.exp(glast[h]) + ku[h]
        on = o[h] * lax.rsqrt(jnp.mean(o[h] * o[h], axis=-1, keepdims=True) + RMS_EPS) * gon
        o_ref[:, hs[h]] = (on * _silu(z_ref[:, hs[h]])).astype(o_ref.dtype)


def _gdn_scan_prompt(qkv, proj, z_col0, beta, gcum, g_onorm, o_sample, Bp, T, H):
    Mp = Bp * T
    Bs = o_sample.shape[0]
    hb = _pick_tile(H, 8, 8) if H % 8 == 0 else H
    tt = CHUNK
    assert T % CHUNK == 0 and (hb % 8 == 0 or hb == H) and Bs <= tt
    nhg, ntt = H // hb, T // tt
    n_chunks = Bp * ntt
    cw = hb * HEAD_DIM
    assert z_col0 % cw == 0
    beta_g = beta.reshape(Mp, nhg, hb).transpose(1, 0, 2)
    gcol_g = gcum.reshape(Mp, nhg, hb).transpose(1, 0, 2)
    grow_g = gcum.reshape(Mp // CHUNK, CHUNK, H).transpose(0, 2, 1)
    body = functools.partial(_gdn_scan_body, hb=hb, chunks_per_seq=ntt, n_chunks=n_chunks)
    row = lambda s: jnp.minimum(s, n_chunks - 1)
    blk = (4 * _nbytes((tt, cw), F32) + 2 * _nbytes((tt, LANES), F32) + _nbytes((hb, LANES), F32)
           + _nbytes((tt, cw), BF16) + _nbytes((Bs, cw), BF16) + _nbytes((hb, HEAD_DIM, HEAD_DIM), F32))
    temporaries = 8 * hb * _nbytes((CHUNK, CHUNK), F32)
    return pl.pallas_call(
        body,
        grid=(nhg, n_chunks + 1),
        in_specs=[
            pl.BlockSpec((tt, cw), lambda h, s: (row(s), h)),
            pl.BlockSpec((tt, cw), lambda h, s: (row(s), nhg + h)),
            pl.BlockSpec((tt, cw), lambda h, s: (row(s), 2 * nhg + h)),
            pl.BlockSpec((tt, cw), lambda h, s: (row(s), z_col0 // cw + h)),
            pl.BlockSpec((1, tt, hb), lambda h, s: (h, row(s), 0)),
            pl.BlockSpec((1, tt, hb), lambda h, s: (h, row(s), 0)),
            pl.BlockSpec((1, hb, CHUNK), lambda h, s: (row(s), h, 0)),
            pl.BlockSpec((1, HEAD_DIM), lambda h, s: (0, 0)),
            pl.BlockSpec((Bs, cw), lambda h, s: (0, h)),
        ],
        out_specs=[
            pl.BlockSpec((tt, cw), lambda h, s: (s, h)),
            pl.BlockSpec((1, hb, HEAD_DIM, HEAD_DIM), lambda h, s: (row(s) // ntt, h, 0, 0)),
        ],
        out_shape=[jax.ShapeDtypeStruct((Mp + Bs, H * HEAD_DIM), BF16),
                   jax.ShapeDtypeStruct((Bp, H, HEAD_DIM, HEAD_DIM), F32)],
        compiler_params=_params(("parallel", "arbitrary"), blk, temporaries),
        name="gdn_scan_prompt",
    )(qkv, qkv, qkv, proj, beta_g, gcol_g, grow_g, g_onorm.reshape(1, HEAD_DIM), o_sample)


def _gdn_step_body(a_ref, beta_ref, s_ref, kt_ref, qt_ref, v_ref, z_ref, gon_ref, so_ref, o_ref, orow_ref, *, n_heads):
    b = pl.program_id(0)
    for h in range(n_heads):
        st = s_ref[0, h]
        a = a_ref[b, h]
        bt = beta_ref[b, h]
        kc = kt_ref[0, :, h:h + 1]
        qc = qt_ref[0, :, h:h + 1]
        sk = jnp.sum(st * kc, axis=0, keepdims=True)
        w = v_ref[0, h:h + 1, :] - a * sk
        sn = a * st + (bt * kc) * w
        so_ref[0, h] = sn
        orow_ref[h:h + 1, :] = jnp.sum(sn * qc, axis=0, keepdims=True)
    o = orow_ref[...]
    on = o * lax.rsqrt(jnp.mean(o * o, axis=-1, keepdims=True) + RMS_EPS) * gon_ref[...]
    o_ref[0] = (on * _silu(z_ref[0])).astype(o_ref.dtype)


def _gdn_step_sample(qkv_s, z_s, beta_s, a_s, state, g_onorm):
    Bs, H = beta_s.shape
    hd = H * HEAD_DIM
    q3 = qkv_s[:, :hd].reshape(Bs, H, HEAD_DIM).transpose(0, 2, 1)
    k3 = qkv_s[:, hd:2 * hd].reshape(Bs, H, HEAD_DIM).transpose(0, 2, 1)
    v3 = qkv_s[:, 2 * hd:].reshape(Bs, H, HEAD_DIM)
    z3 = z_s.reshape(Bs, H, HEAD_DIM)
    smem = pl.BlockSpec(memory_space=pltpu.SMEM)
    body = functools.partial(_gdn_step_body, n_heads=H)
    blk = 2 * _nbytes((H, HEAD_DIM, HEAD_DIM), F32) + 6 * _nbytes((HEAD_DIM, LANES), F32)
    so, o = pl.pallas_call(
        body,
        grid=(Bs,),
        in_specs=[smem, smem,
                  pl.BlockSpec((1, H, HEAD_DIM, HEAD_DIM), lambda b: (b, 0, 0, 0)),
                  pl.BlockSpec((1, HEAD_DIM, H), lambda b: (b, 0, 0)),
                  pl.BlockSpec((1, HEAD_DIM, H), lambda b: (b, 0, 0)),
                  pl.BlockSpec((1, H, HEAD_DIM), lambda b: (b, 0, 0)),
                  pl.BlockSpec((1, H, HEAD_DIM), lambda b: (b, 0, 0)),
                  pl.BlockSpec((1, HEAD_DIM), lambda b: (0, 0))],
        out_specs=[pl.BlockSpec((1, H, HEAD_DIM, HEAD_DIM), lambda b: (b, 0, 0, 0)),
                   pl.BlockSpec((1, H, HEAD_DIM), lambda b: (b, 0, 0))],
        out_shape=[jax.ShapeDtypeStruct((Bs, H, HEAD_DIM, HEAD_DIM), F32),
                   jax.ShapeDtypeStruct((Bs, H, HEAD_DIM), BF16)],
        scratch_shapes=[pltpu.VMEM((H, HEAD_DIM), F32)],
        compiler_params=_params(("arbitrary",), blk),
        name="gdn_step_sample",
    )(a_s, beta_s, state, k3, q3, v3, z3, g_onorm.reshape(1, HEAD_DIM))
    return o.reshape(Bs, hd), so


def _seq_tails(a, Bp, T, n, ncols):
    return jnp.stack([lax.slice(a, (b * T + T - n, 0), ((b + 1) * T, ncols)) for b in range(Bp)])


def _wcol_bytes(*ws):
    return sum(w.shape[1] * w.dtype.itemsize for w in ws)


def _conformer_layer(h, u, Mp, Bp, T, Bs, state, j, w_pw1, w_dw, b_dw, ln_g, ln_b, w_pw2):
    M, D = h.shape
    Dc = w_pw1.shape[2] // 2
    tm, tn = _mm_tiles(M, D, Dc, 2 * _wcol_bytes(w_pw1))
    glu = _matmul(_mm_glu_body, "mm_glu", M, Dc, tm, tn,
                  [_x_spec(u, tm), _w_spec(w_pw1, j, tn), _w_spec(w_pw1, j, tn, Dc // tn)], F32)
    c_s, new_s = _conf_conv_sample(glu, Mp, Bs, state, w_dw, b_dw, ln_g, ln_b)
    c = _conf_conv_prompt(glu, Bp, T, c_s, w_dw, b_dw, ln_g, ln_b)
    tm, tn = _mm_tiles(M, Dc, D, _wcol_bytes(w_pw2))
    h = _matmul(_mm_res_body, "mm_pw2", M, D, tm, tn, [_x_spec(c, tm), _w_spec(w_pw2, j, tn), _r_spec(h, tm, tn)], F32)
    new_p = _seq_tails(glu, Bp, T, w_dw.shape[0] - 1, Dc)
    return h, new_p, new_s


def _gdn_layer(h, u, Mp, Bp, T, Bs, conv_state, gdn_state, j, w_in, w_conv, a_log, dt_bias, g_onorm, w_out):
    M, D = h.shape
    H = a_log.shape[0]
    hd = H * HEAD_DIM
    qkv_dim = 3 * hd
    n_main = qkv_dim + hd
    tm, tn = _mm_tiles(M, D, n_main, _wcol_bytes(w_in))
    proj = _matmul(_mm_plain_body, "mm_gdn_in", M, n_main, tm, tn, [_x_spec(u, tm), _w_spec(w_in, j, tn)], F32)
    w_gate = w_in[:, :, n_main:]
    logits = _matmul(_mm_plain_body, "mm_gdn_gates", M, 2 * H, tm, 2 * H,
                     [_x_spec(u, tm), _w_spec(w_gate, j, 2 * H)], F32)

    beta_p, gcum_p = _gates(logits, a_log, dt_bias, 0, Mp, CHUNK)
    beta_s, a_s = _gates(logits, a_log, dt_bias, Mp, Bs, 1)

    qkv_s, new_conv_s = _qkv_prep_sample(proj, Mp, Bs, qkv_dim, conv_state, w_conv)
    o_s, s_s = _gdn_step_sample(qkv_s, lax.slice(proj, (Mp, qkv_dim), (M, n_main)), beta_s, a_s, gdn_state, g_onorm)

    qkv_p = _qkv_prep_prompt(proj, Bp, T, qkv_dim, w_conv)
    o, s_p = _gdn_scan_prompt(qkv_p, proj, qkv_dim, beta_p, gcum_p, g_onorm, o_s, Bp, T, H)
    tm, tn = _mm_tiles(M, hd, D, _wcol_bytes(w_out))
    h = _matmul(_mm_res_body, "mm_gdn_out", M, D, tm, tn, [_x_spec(o, tm), _w_spec(w_out, j, tn), _r_spec(h, tm, tn)], F32)
    new_conv_p = _seq_tails(proj, Bp, T, w_conv.shape[0] - 1, qkv_dim)
    return h, new_conv_p, new_conv_s, s_p, s_s


def _ffn_ple(h, p_all, i, g_ffn, w_gate, w_up, w_down, g_ple, w_pg, w_pp):
    M, D = h.shape
    F = w_gate.shape[2]
    u = _rmsnorm(h, g_ffn, BF16)
    tm, tn = _mm_tiles(M, D, F, _wcol_bytes(w_gate, w_up))
    act = _matmul(_mm_swiglu_body, "mm_ffn_up", M, F, tm, tn,
                  [_x_spec(u, tm), _w_spec(w_gate, i, tn), _w_spec(w_up, i, tn)], BF16)
    tm, tn = _mm_tiles(M, F, D, _wcol_bytes(w_down))
    h = _matmul(_mm_res_body, "mm_ffn_down", M, D, tm, tn,
                [_x_spec(act, tm), _w_spec(w_down, i, tn), _r_spec(h, tm, tn)], F32)
    u = _rmsnorm(h, g_ple, BF16)
    tm, tn = _mm_tiles(M, D, D, _wcol_bytes(w_pg, w_pp))
    p_spec = (p_all, (None, tm, p_all.shape[2]), lambda r, c: (i, r, 0))
    h = _matmul(_mm_ple_body, "mm_ple", M, D, tm, tn,
                [_x_spec(u, tm), p_spec, _w_spec(w_pg, i, tn), _w_spec(w_pp, i, tn), _r_spec(h, tm, tn)], F32)
    return h


def kernel(x_prompt, x_sample, p_prompt, p_sample, state_conv_conformer, state_conv_qkv, state_gdn, g_mix, g_ffn, g_ple, g_final, conf_w_pw1, conf_w_dw, conf_b_dw, conf_ln_g, conf_ln_b, conf_w_pw2, gdn_w_in, gdn_w_conv, gdn_a_log, gdn_dt_bias, gdn_g_onorm, gdn_w_out, ffn_w_gate, ffn_w_up, ffn_w_down, ple_w_gate, ple_w_proj):
    Bp, T, D = x_prompt.shape
    Bs, Ts, _ = x_sample.shape
    assert Ts == 1
    depth = g_mix.shape[0]
    Mp = Bp * T
    M = Mp + Bs
    h = jnp.concatenate([x_prompt.reshape(Mp, D), x_sample.reshape(Bs, D)], axis=0)
    p_all = jnp.concatenate([p_prompt.reshape(depth, Mp, -1), p_sample.reshape(depth, Bs, -1)], axis=1).astype(BF16)
    ffn_w_down16 = ffn_w_down.astype(BF16)

    conf_p, conf_s, qkv_p, qkv_s, gdn_p, gdn_s = [], [], [], [], [], []
    for i in range(depth):
        j = i // 2
        u = _rmsnorm(h, g_mix[i], BF16)
        if i % 2 == 0:
            h, np_, ns_ = _conformer_layer(h, u, Mp, Bp, T, Bs, state_conv_conformer[j], j, conf_w_pw1, conf_w_dw[j],
                                           conf_b_dw[j], conf_ln_g[j], conf_ln_b[j], conf_w_pw2)
            conf_p.append(np_)
            conf_s.append(ns_)
        else:
            h, cp_, cs_, sp_, ss_ = _gdn_layer(h, u, Mp, Bp, T, Bs, state_conv_qkv[j], state_gdn[j], j, gdn_w_in,
                                               gdn_w_conv[j], gdn_a_log[j], gdn_dt_bias[j], gdn_g_onorm[j], gdn_w_out)
            qkv_p.append(cp_)
            qkv_s.append(cs_)
            gdn_p.append(sp_)
            gdn_s.append(ss_)
        h = _ffn_ple(h, p_all, i, g_ffn[i], ffn_w_gate, ffn_w_up, ffn_w_down16, g_ple[i], ple_w_gate, ple_w_proj)

    y_p = _rmsnorm(h, g_final, F32, 0, Mp).reshape(Bp, T, D)
    y_s = _rmsnorm(h, g_final, F32, Mp, Bs).reshape(Bs, 1, D)
    return (y_p, y_s, jnp.stack(conf_p), jnp.stack(qkv_p), jnp.stack(gdn_p),
            jnp.stack(conf_s), jnp.stack(qkv_s), jnp.stack(gdn_s))
```

```python
import functools

import jax
import jax.numpy as jnp
from jax import lax
from jax.experimental import pallas as pl
from jax.experimental.pallas import tpu as pltpu

F32 = jnp.float32
BF16 = jnp.bfloat16

RMS_EPS = 1e-6
LN_EPS = 1e-5
L2_EPS = 1e-6
CHUNK = 256
HEAD_DIM = 128
LANES = 128
SUBLANES = 8
V7X_VMEM_LIMIT_CAP = 60000 * 1024
VMEM_INTERNAL_SCRATCH = 8 * 2**20


def _pick_tile(n, cap, mult):
    best = None
    for d in range(mult, min(n, cap) + 1, mult):
        if n % d == 0:
            best = d
    return n if best is None else best


def _params(sem, block_bytes, scratch_bytes=0):
    need = 2 * block_bytes + scratch_bytes + VMEM_INTERNAL_SCRATCH
    return pltpu.CompilerParams(dimension_semantics=sem,
                                vmem_limit_bytes=int(min(max(need, 32 * 2**20), V7X_VMEM_LIMIT_CAP)))


def _nbytes(shape, dtype):
    n = 1
    for s in shape:
        n *= 1 if s is None else s
    return n * jnp.dtype(dtype).itemsize


def _bdot(a, b):
    return jnp.dot(a.astype(BF16), b.astype(BF16), preferred_element_type=F32)


def _silu(x):
    return x * jax.nn.sigmoid(x)


def _rms_body(x_ref, g_ref, o_ref):
    x = x_ref[...]
    y = x * lax.rsqrt(jnp.mean(x * x, axis=-1, keepdims=True) + RMS_EPS)
    o_ref[...] = (y * g_ref[...]).astype(o_ref.dtype)


def _rmsnorm(x, g, out_dtype, row0=0, rows=None):
    M, D = x.shape
    rows = M if rows is None else rows
    tm = _pick_tile(rows, 320, 16)
    assert row0 % tm == 0
    blk = _nbytes((tm, D), F32) + _nbytes((tm, D), out_dtype)
    return pl.pallas_call(
        _rms_body,
        grid=(rows // tm,),
        in_specs=[pl.BlockSpec((tm, D), lambda i: (i + row0 // tm, 0)),
                  pl.BlockSpec((1, D), lambda i: (0, 0))],
        out_specs=pl.BlockSpec((tm, D), lambda i: (i, 0)),
        out_shape=jax.ShapeDtypeStruct((rows, D), out_dtype),
        compiler_params=_params(("parallel",), blk),
        name="rmsnorm",
    )(x, g.reshape(1, D))


def _embed_body(xp_ref, xs_ref, g_ref, h_ref, xg_ref, ss_ref, *, n_prompt_tiles):
    def emit(x):
        h_ref[...] = x
        xg_ref[...] = (x * g_ref[...]).astype(xg_ref.dtype)
        ss_ref[...] = jnp.sum(x * x, axis=-1, keepdims=True)

    @pl.when(pl.program_id(0) < n_prompt_tiles)
    def _():
        emit(xp_ref[...])

    @pl.when(pl.program_id(0) >= n_prompt_tiles)
    def _():
        emit(xs_ref[...])


def _embed(xp, xs, g):
    (Mp, D), Bs = xp.shape, xs.shape[0]
    tr = Bs
    assert Mp % tr == 0 and tr % SUBLANES == 0
    npt = Mp // tr
    body = functools.partial(_embed_body, n_prompt_tiles=npt)
    blk = 3 * _nbytes((tr, D), F32) + _nbytes((tr, D), BF16) + _nbytes((tr, LANES), F32)
    h, xg, ss = pl.pallas_call(
        body,
        grid=(npt + 1,),
        in_specs=[pl.BlockSpec((tr, D), lambda i: (jnp.minimum(i, npt - 1), 0)),
                  pl.BlockSpec((tr, D), lambda i: (0, 0)),
                  pl.BlockSpec((1, D), lambda i: (0, 0))],
        out_specs=[pl.BlockSpec((tr, D), lambda i: (i, 0)), pl.BlockSpec((tr, D), lambda i: (i, 0)),
                   pl.BlockSpec((tr, 1), lambda i: (i, 0))],
        out_shape=[jax.ShapeDtypeStruct((Mp + Bs, D), F32), jax.ShapeDtypeStruct((Mp + Bs, D), BF16),
                   jax.ShapeDtypeStruct((Mp + Bs, 1), F32)],
        compiler_params=_params(("parallel",), blk),
        name="embed_norm",
    )(xp, xs, g.reshape(1, D))
    return h, (xg, ss)


def _wdot(x_ref, w_ref):
    return jnp.dot(x_ref[...], w_ref[...].astype(BF16), preferred_element_type=F32)


def _row_scale(ss_ref, width):
    return lax.rsqrt(ss_ref[...] * (1.0 / width) + RMS_EPS)


def _emit_normalised(val, g_ref, o_ref, xg_ref, ss_ref):
    o_ref[...] = val
    xg_ref[...] = (val * g_ref[...]).astype(xg_ref.dtype)
    part = jnp.sum(val * val, axis=-1, keepdims=True)
    j = pl.program_id(1)

    @pl.when(j == 0)
    def _():
        ss_ref[...] = part

    @pl.when(j > 0)
    def _():
        ss_ref[...] += part


def _mm_glu_body(x_ref, ss_ref, wa_ref, wg_ref, o_ref):
    r = _row_scale(ss_ref, x_ref.shape[1])
    o_ref[...] = ((_wdot(x_ref, wa_ref) * r) * jax.nn.sigmoid(_wdot(x_ref, wg_ref) * r)).astype(o_ref.dtype)


def _mm_swiglu_body(x_ref, ss_ref, wg_ref, wu_ref, o_ref):
    r = _row_scale(ss_ref, x_ref.shape[1])
    o_ref[...] = (_silu(_wdot(x_ref, wg_ref) * r) * (_wdot(x_ref, wu_ref) * r)).astype(o_ref.dtype)


def _mm_plain_body(x_ref, ss_ref, w_ref, o_ref):
    o_ref[...] = (_wdot(x_ref, w_ref) * _row_scale(ss_ref, x_ref.shape[1])).astype(o_ref.dtype)


def _mm_plain_wt_body(x_ref, ss_ref, wt_ref, o_ref):
    d = lax.dot_general(x_ref[...], wt_ref[...].astype(BF16), (((1,), (1,)), ((), ())), preferred_element_type=F32)
    o_ref[...] = (d * _row_scale(ss_ref, x_ref.shape[1])).astype(o_ref.dtype)


def _mm_res_body(x_ref, w_ref, r_ref, g_ref, o_ref, xg_ref, sso_ref):
    _emit_normalised(r_ref[...] + _wdot(x_ref, w_ref), g_ref, o_ref, xg_ref, sso_ref)


def _ple_value(x_ref, ss_ref, p_ref, wg_ref, wp_ref, r_ref):
    gate = jax.nn.sigmoid(_wdot(x_ref, wg_ref) * _row_scale(ss_ref, x_ref.shape[1]))
    return r_ref[...] + gate * _wdot(p_ref, wp_ref)


def _mm_ple_body(x_ref, ss_ref, p_ref, wg_ref, wp_ref, r_ref, g_ref, o_ref, xg_ref, sso_ref):
    _emit_normalised(_ple_value(x_ref, ss_ref, p_ref, wg_ref, wp_ref, r_ref), g_ref, o_ref, xg_ref, sso_ref)


def _mm_ple_last_body(x_ref, ss_ref, p_ref, wg_ref, wp_ref, r_ref, o_ref):
    o_ref[...] = _ple_value(x_ref, ss_ref, p_ref, wg_ref, wp_ref, r_ref)


def _matmul(body, name, M, N, tm, tn, operands, out_dtype, emit_normalised=False):
    blk = sum(_nbytes(bs, a.dtype) for a, bs, _ in operands) + _nbytes((tm, tn), out_dtype)
    out_specs = [pl.BlockSpec((tm, tn), lambda i, j: (i, j))]
    out_shape = [jax.ShapeDtypeStruct((M, N), out_dtype)]
    if emit_normalised:
        out_specs += [pl.BlockSpec((tm, tn), lambda i, j: (i, j)), pl.BlockSpec((tm, 1), lambda i, j: (i, 0))]
        out_shape += [jax.ShapeDtypeStruct((M, N), BF16), jax.ShapeDtypeStruct((M, 1), F32)]
        blk += _nbytes((tm, tn), BF16) + _nbytes((tm, LANES), F32)
    outs = pl.pallas_call(
        body,
        grid=(M // tm, N // tn),
        in_specs=[pl.BlockSpec(bs, im) for _, bs, im in operands],
        out_specs=out_specs,
        out_shape=out_shape,
        compiler_params=_params(("parallel", "arbitrary"), blk),
        name=name,
    )(*[a for a, _, _ in operands])
    return (outs[0], (outs[1], outs[2])) if emit_normalised else outs[0]


def _x_spec(x, tm):
    return (x, (tm, x.shape[1]), lambda i, j: (i, 0))


def _xn_specs(xn, tm):
    xg, ss = xn
    return [_x_spec(xg, tm), (ss, (tm, 1), lambda i, j: (i, 0))]


def _w_spec(w, layer, tn, col_block0=0):
    return (w, (None, w.shape[1], tn), lambda i, j: (layer, 0, j + col_block0))


def _r_spec(r, tm, tn):
    return (r, (tm, tn), lambda i, j: (i, j))


def _g_spec(g, tn):
    return (g.reshape(1, -1), (1, tn), lambda i, j: (0, j))


def _mm_tiles(M, K, N, w_bytes_per_col):
    tn = _pick_tile(N, 512 if 512 * w_bytes_per_col <= 8 * 2**20 else 256, LANES)
    tm_cap = 1040 if K <= 4096 else 640
    return _pick_tile(M, tm_cap, 16), tn


def _conv_blocks(load_blk, w_row, n_taps, n_out, width):
    qmax = (n_taps - 1) // SUBLANES
    rows = lax.broadcasted_iota(jnp.int32, (SUBLANES, width), 0)
    x = {k: load_blk(k) for k in range(-(qmax + 1), n_out)}
    acc = [None] * n_out
    for r in range(min(SUBLANES, n_taps)):
        if r == 0:
            y = x
        else:
            rot = {k: pltpu.roll(v, r, 0) for k, v in x.items()}
            y = {k: jnp.where(rows >= r, rot[k], rot[k - 1]) for k in range(-qmax, n_out)}
        for q in range(qmax + 1):
            d = SUBLANES * q + r
            if d >= n_taps:
                continue
            w = w_row(d)
            for j in range(n_out):
                t = y[j - q] * w
                acc[j] = t if acc[j] is None else acc[j] + t
    return acc


CONF_HALO = 32
CONV_STRIP = 8
CONV_ROWS_CAP = 256
QKV_ROWS_CAP = 512


def _conf_conv_body(x_ref, halo_ref, cs_ref, w_ref, b_ref, lg_ref, lb_ref, o_ref, xe_ref, y_ref, *,
                    tb, ncb, n_taps, tiles_per_seq, n_tiles):
    step = pl.program_id(0)

    @pl.when(step == n_tiles)
    def _():
        o_ref[0:cs_ref.shape[0], :] = cs_ref[...].astype(o_ref.dtype)

    @pl.when(step < n_tiles)
    def _():
        _conf_conv_tile(x_ref, halo_ref, w_ref, b_ref, lg_ref, lb_ref, o_ref, xe_ref, y_ref,
                        first=step % tiles_per_seq == 0, tb=tb, ncb=ncb, n_taps=n_taps)


def _conf_conv_tile(x_ref, halo_ref, w_ref, b_ref, lg_ref, lb_ref, o_ref, xe_ref, y_ref, *, first, tb, ncb, n_taps):
    for cb in range(ncb):
        sl = slice(cb * LANES, (cb + 1) * LANES)
        xe_ref[cb, CONF_HALO:CONF_HALO + tb, :] = x_ref[:, sl]
        xe_ref[cb, 0:CONF_HALO, :] = jnp.where(first, 0.0, halo_ref[:, sl])

    strip_rows = SUBLANES * CONV_STRIP
    n_strips = tb // strip_rows

    def col_loop(cb, carry):
        def strip_loop(s, carry2):
            r0 = pl.multiple_of(s * strip_rows, strip_rows)
            acc = _conv_blocks(
                lambda k: xe_ref[cb, pl.ds(r0 + (CONF_HALO + SUBLANES * k), SUBLANES), :],
                lambda d: w_ref[cb, pl.ds(n_taps - 1 - d, 1), :],
                n_taps, CONV_STRIP, LANES)
            bias = b_ref[cb]
            for j in range(CONV_STRIP):
                y_ref[cb, pl.ds(r0 + SUBLANES * j, SUBLANES), :] = acc[j] + bias
            return carry2
        return lax.fori_loop(0, n_strips, strip_loop, carry)

    lax.fori_loop(0, ncb, col_loop, 0)

    ln_rows = 32
    inv_c = 1.0 / (ncb * LANES)

    def ln_loop(s, carry):
        r0 = pl.multiple_of(s * ln_rows, ln_rows)
        y = y_ref[:, pl.ds(r0, ln_rows), :]
        mu = jnp.sum(jnp.sum(y, axis=0), axis=-1, keepdims=True) * inv_c
        yc = y - mu[None]
        var = jnp.sum(jnp.sum(yc * yc, axis=0), axis=-1, keepdims=True) * inv_c
        rstd = lax.rsqrt(var + LN_EPS)
        for cb in range(ncb):
            t = yc[cb] * rstd * lg_ref[cb] + lb_ref[cb]
            o_ref[pl.ds(r0, ln_rows), cb * LANES:(cb + 1) * LANES] = _silu(t).astype(o_ref.dtype)
        return carry

    lax.fori_loop(0, tb // ln_rows, ln_loop, 0)


def _col_major(v, pad_rows=None):
    R, C = v.shape
    out = v.reshape(R, C // LANES, LANES).transpose(1, 0, 2)
    if pad_rows is not None and pad_rows > R:
        out = jnp.pad(out, ((0, 0), (0, pad_rows - R), (0, 0)))
    return out


def _conf_conv_prompt(glu, Bp, T, c_sample, w_dw, b_dw, ln_g, ln_b):
    Dc = glu.shape[1]
    Bs = c_sample.shape[0]
    n_taps = w_dw.shape[0]
    assert n_taps - 1 <= CONF_HALO and T >= CONF_HALO
    tb = _pick_tile(T, CONV_ROWS_CAP, SUBLANES * CONV_STRIP)
    assert Bs <= tb
    ncb = Dc // LANES
    n_tiles = Bp * T // tb
    body = functools.partial(_conf_conv_body, tb=tb, ncb=ncb, n_taps=n_taps, tiles_per_seq=T // tb, n_tiles=n_tiles)
    hb = tb // CONF_HALO
    blk = (_nbytes((tb, Dc), F32) + _nbytes((CONF_HALO, Dc), F32) + _nbytes((tb, Dc), BF16) + _nbytes((Bs, Dc), F32)
           + _nbytes((ncb, 32 + 3, LANES), F32))
    scratch = _nbytes((ncb, tb + CONF_HALO, LANES), F32) + _nbytes((ncb, tb, LANES), F32)
    tile = lambda s: jnp.minimum(s, n_tiles - 1)
    const3 = lambda s: (0, 0, 0)
    return pl.pallas_call(
        body,
        grid=(n_tiles + 1,),
        in_specs=[
            pl.BlockSpec((tb, Dc), lambda s: (tile(s), 0)),
            pl.BlockSpec((CONF_HALO, Dc), lambda s: (jnp.maximum(tile(s) * hb - 1, 0), 0)),
            pl.BlockSpec((Bs, Dc), lambda s: (0, 0)),
            pl.BlockSpec((ncb, 32, LANES), const3),
            pl.BlockSpec((ncb, 1, LANES), const3),
            pl.BlockSpec((ncb, 1, LANES), const3),
            pl.BlockSpec((ncb, 1, LANES), const3),
        ],
        out_specs=pl.BlockSpec((tb, Dc), lambda s: (s, 0)),
        out_shape=jax.ShapeDtypeStruct((Bp * T + Bs, Dc), BF16),
        scratch_shapes=[pltpu.VMEM((ncb, tb + CONF_HALO, LANES), F32), pltpu.VMEM((ncb, tb, LANES), F32)],
        compiler_params=_params(("arbitrary",), blk, scratch),
        name="conf_conv_prompt",
    )(glu, glu, c_sample, _col_major(w_dw, 32), _col_major(b_dw[None]), _col_major(ln_g[None]), _col_major(ln_b[None]))


def _conf_conv_sample_body(st_ref, x_ref, w_ref, b_ref, lg_ref, lb_ref, o_ref, so_ref, acc_ref, *, n_prev):
    w = pl.program_id(0)

    @pl.when(w == 0)
    def _():
        acc_ref[...] = x_ref[...] * w_ref[pl.ds(n_prev, 1), :]

    @pl.when(w < n_prev)
    def _():
        st = st_ref[...]
        acc_ref[...] += st * w_ref[pl.ds(w, 1), :]
        so_ref[...] = st

    @pl.when(w == n_prev)
    def _():
        so_ref[...] = x_ref[...]
        y = acc_ref[...] + b_ref[...]
        mu = jnp.mean(y, axis=-1, keepdims=True)
        yc = y - mu
        rstd = lax.rsqrt(jnp.mean(yc * yc, axis=-1, keepdims=True) + LN_EPS)
        o_ref[...] = _silu(yc * rstd * lg_ref[...] + lb_ref[...])


def _conf_conv_sample(glu, Mp, Bs, state_t, w_dw, b_dw, ln_g, ln_b):
    Dc = glu.shape[1]
    n_taps = w_dw.shape[0]
    n_prev = n_taps - 1
    assert Mp % Bs == 0
    blk = 4 * _nbytes((Bs, Dc), F32) + _nbytes((n_taps + 3 * SUBLANES, Dc), F32)
    row = lambda w: (0, 0)
    body = functools.partial(_conf_conv_sample_body, n_prev=n_prev)
    return pl.pallas_call(
        body,
        grid=(n_prev + 1,),
        in_specs=[
            pl.BlockSpec((None, Bs, Dc), lambda w: (jnp.minimum(w, n_prev - 1), 0, 0)),
            pl.BlockSpec((Bs, Dc), lambda w: (Mp // Bs, 0)),
            pl.BlockSpec((n_taps, Dc), row),
            pl.BlockSpec((1, Dc), row), pl.BlockSpec((1, Dc), row), pl.BlockSpec((1, Dc), row),
        ],
        out_specs=[pl.BlockSpec((Bs, Dc), lambda w: (0, 0)),
                   pl.BlockSpec((None, Bs, Dc), lambda w: (jnp.maximum(w - 1, 0), 0, 0))],
        out_shape=[jax.ShapeDtypeStruct((Bs, Dc), F32), jax.ShapeDtypeStruct((n_prev, Bs, Dc), F32)],
        scratch_shapes=[pltpu.VMEM((Bs, Dc), F32)],
        compiler_params=_params(("arbitrary",), blk, _nbytes((Bs, Dc), F32)),
        name="conf_conv_sample",
    )(state_t, glu, w_dw, b_dw[None], ln_g[None], ln_b[None])


def _qkv_finish(y, kind):
    y = _silu(y)
    ss = jnp.sum(y * y, axis=-1, keepdims=True)
    qscale = jnp.where(kind == 0, HEAD_DIM ** -0.5, 1.0).astype(F32)
    fac = jnp.where(kind == 2, 1.0, lax.rsqrt(ss + L2_EPS) * qscale)
    return y * fac


QKV_HALO = 8


def _qkv_prep_body(x_ref, halo_ref, w_ref, o_ref, xe_ref, *, tb, cw, n_taps, groups_per_kind):
    i = pl.program_id(1)
    kind = pl.program_id(2) // groups_per_kind
    xe_ref[QKV_HALO:QKV_HALO + tb, :] = x_ref[...]
    xe_ref[0:QKV_HALO, :] = jnp.where(i == 0, 0.0, halo_ref[...])
    strip_rows = SUBLANES * CONV_STRIP

    def strip_loop(s, carry):
        r0 = pl.multiple_of(s * strip_rows, strip_rows)
        for cb in range(cw // LANES):
            sl = slice(cb * LANES, (cb + 1) * LANES)
            acc = _conv_blocks(
                lambda k: xe_ref[pl.ds(r0 + (QKV_HALO + SUBLANES * k), SUBLANES), sl],
                lambda d: w_ref[pl.ds(n_taps - 1 - d, 1), sl],
                n_taps, CONV_STRIP, LANES)
            for j in range(CONV_STRIP):
                o_ref[pl.ds(r0 + SUBLANES * j, SUBLANES), sl] = _qkv_finish(acc[j], kind)
        return carry

    lax.fori_loop(0, tb // strip_rows, strip_loop, 0)


def _qkv_prep_prompt(proj, Bp, T, qkv_dim, w_conv):
    n_taps = w_conv.shape[0]
    assert n_taps - 1 <= QKV_HALO
    tb = _pick_tile(T, QKV_ROWS_CAP, SUBLANES * CONV_STRIP)
    cw = _pick_tile(qkv_dim // 3, 1024, LANES)
    nt = T // tb
    ncg = qkv_dim // cw
    body = functools.partial(_qkv_prep_body, tb=tb, cw=cw, n_taps=n_taps, groups_per_kind=ncg // 3)
    hb = tb // QKV_HALO
    blk = 2 * _nbytes((tb, cw), F32) + 2 * _nbytes((QKV_HALO, cw), F32)
    return pl.pallas_call(
        body,
        grid=(Bp, nt, ncg),
        in_specs=[
            pl.BlockSpec((tb, cw), lambda b, i, c: (b * nt + i, c)),
            pl.BlockSpec((QKV_HALO, cw), lambda b, i, c: (jnp.maximum((b * nt + i) * hb - 1, 0), c)),
            pl.BlockSpec((n_taps, cw), lambda b, i, c: (0, c)),
        ],
        out_specs=pl.BlockSpec((tb, cw), lambda b, i, c: (b * nt + i, c)),
        out_shape=jax.ShapeDtypeStruct((Bp * T, qkv_dim), F32),
        scratch_shapes=[pltpu.VMEM((tb + QKV_HALO, cw), F32)],
        compiler_params=_params(("parallel", "parallel", "parallel"), blk, _nbytes((tb + QKV_HALO, cw), F32)),
        name="qkv_prep_prompt",
    )(proj, proj, w_conv)


def _qkv_prep_sample_body(st_ref, x_ref, w_ref, o_ref, so_ref, *, n_prev, cw, groups_per_kind):
    kind = pl.program_id(0) // groups_per_kind
    x = x_ref[...]
    y = x * w_ref[n_prev:n_prev + 1, :]
    for w in range(n_prev):
        y = y + st_ref[w] * w_ref[w:w + 1, :]
    for w in range(1, n_prev):
        so_ref[w - 1] = st_ref[w]
    so_ref[n_prev - 1] = x
    for cb in range(cw // LANES):
        sl = slice(cb * LANES, (cb + 1) * LANES)
        o_ref[:, sl] = _qkv_finish(y[:, sl], kind)


def _qkv_prep_sample(proj, Mp, Bs, qkv_dim, state_t, w_conv):
    n_prev = w_conv.shape[0] - 1
    cw = _pick_tile(qkv_dim // 3, 1024, LANES)
    ncg = qkv_dim // cw
    assert Mp % Bs == 0
    body = functools.partial(_qkv_prep_sample_body, n_prev=n_prev, cw=cw, groups_per_kind=ncg // 3)
    blk = 2 * _nbytes((n_prev, Bs, cw), F32) + 2 * _nbytes((Bs, cw), F32) + _nbytes((SUBLANES, cw), F32)
    return pl.pallas_call(
        body,
        grid=(ncg,),
        in_specs=[pl.BlockSpec((n_prev, Bs, cw), lambda c: (0, 0, c)),
                  pl.BlockSpec((Bs, cw), lambda c: (Mp // Bs, c)),
                  pl.BlockSpec((n_prev + 1, cw), lambda c: (0, c))],
        out_specs=[pl.BlockSpec((Bs, cw), lambda c: (0, c)), pl.BlockSpec((n_prev, Bs, cw), lambda c: (0, 0, c))],
        out_shape=[jax.ShapeDtypeStruct((Bs, qkv_dim), F32), jax.ShapeDtypeStruct((n_prev, Bs, qkv_dim), F32)],
        compiler_params=_params(("parallel",), blk),
        name="qkv_prep_sample",
    )(state_t, proj, w_conv)


def _gates_body(l_ref, alog_ref, dtb_ref, beta_ref, g_ref, *, n_heads, rows, chunk):
    lg = l_ref[...]
    beta_ref[...] = jax.nn.sigmoid(lg[:, :n_heads])
    g = -jnp.exp(alog_ref[...]) * jax.nn.softplus(lg[:, n_heads:] + dtb_ref[...])
    if chunk == 1:
        g_ref[...] = jnp.exp(g)
    else:
        tril = (lax.broadcasted_iota(jnp.int32, (chunk, chunk), 0)
                >= lax.broadcasted_iota(jnp.int32, (chunk, chunk), 1)).astype(F32)
        for c in range(rows // chunk):
            g_ref[c * chunk:(c + 1) * chunk, :] = jnp.dot(
                tril, g[c * chunk:(c + 1) * chunk, :], preferred_element_type=F32, precision=lax.Precision.HIGHEST)


def _gates(logits, a_log, dt_bias, row0, rows, chunk):
    H = logits.shape[1] // 2
    tr = _pick_tile(rows, 512, max(chunk, SUBLANES))
    assert row0 % tr == 0
    body = functools.partial(_gates_body, n_heads=H, rows=tr, chunk=chunk)
    return pl.pallas_call(
        body,
        grid=(rows // tr,),
        in_specs=[pl.BlockSpec((tr, 2 * H), lambda i: (i + row0 // tr, 0)),
                  pl.BlockSpec((1, H), lambda i: (0, 0)), pl.BlockSpec((1, H), lambda i: (0, 0))],
        out_specs=[pl.BlockSpec((tr, H), lambda i: (i, 0)), pl.BlockSpec((tr, H), lambda i: (i, 0))],
        out_shape=[jax.ShapeDtypeStruct((rows, H), F32), jax.ShapeDtypeStruct((rows, H), F32)],
        compiler_params=_params(("parallel",), 4 * _nbytes((tr, LANES), F32)),
        name="gdn_gates",
    )(logits, a_log.reshape(1, H), dt_bias.reshape(1, H))


def _gdn_scan_body(q_ref, k_ref, v_ref, z_ref, beta_ref, g_ref, grow_ref, gon_ref, os_ref, o_ref, s_ref, *,
                   hb, chunks_per_seq, n_chunks):
    step = pl.program_id(1)

    @pl.when(step == n_chunks)
    def _():
        o_ref[0:os_ref.shape[0], :] = os_ref[...]

    @pl.when(step < n_chunks)
    def _():
        @pl.when(step % chunks_per_seq == 0)
        def _():
            s_ref[...] = jnp.zeros_like(s_ref)

        _gdn_chunk(q_ref, k_ref, v_ref, z_ref, beta_ref, g_ref, grow_ref, gon_ref, o_ref, s_ref, hb=hb)


def _gdn_chunk(q_ref, k_ref, v_ref, z_ref, beta_ref, g_ref, grow_ref, gon_ref, o_ref, s_ref, *, hb):
    C = CHUNK

    ri = lax.broadcasted_iota(jnp.int32, (C, C), 0)
    ci = lax.broadcasted_iota(jnp.int32, (C, C), 1)
    causal = ri >= ci
    strict = ri > ci
    eye = (ri == ci).astype(F32)
    blk = lambda s: (ri // s) == (ci // s)
    gon = gon_ref[...]

    heads = range(hb)
    hs = [slice(h * HEAD_DIM, (h + 1) * HEAD_DIM) for h in heads]
    nt = (((1,), (1,)), ((), ()))
    tn = (((0,), (0,)), ((), ()))
    b16 = lambda xs: [x.astype(BF16) for x in xs]
    mm = lambda xs, ys: [jnp.dot(x, y, preferred_element_type=F32) for x, y in zip(xs, ys)]

    q = [q_ref[:, hs[h]] for h in heads]
    k = [k_ref[:, hs[h]] for h in heads]
    bcol = [beta_ref[0, :, h:h + 1] for h in heads]
    gcol = [g_ref[0, :, h:h + 1] for h in heads]
    grow = [grow_ref[0, h:h + 1, :] for h in heads]
    decay = [jnp.where(causal, jnp.exp(jnp.where(causal, gcol[h] - grow[h], 0.0)), 0.0) for h in heads]
    eg = [jnp.exp(gcol[h]) for h in heads]
    kb = [k[h] * bcol[h] for h in heads]
    k16 = b16(k)
    kk = [lax.dot_general(x, y, nt, preferred_element_type=F32) for x, y in zip(b16(kb), k16)]
    qk = [lax.dot_general(x, y, nt, preferred_element_type=F32) for x, y in zip(b16(q), k16)]
    bm = [jnp.where(strict, kk[h] * decay[h], 0.0) for h in heads]
    attn16 = b16([qk[h] * decay[h] for h in heads])

    in8 = blk(SUBLANES)
    x0 = [jnp.where(in8, -bm[h], 0.0) for h in heads]
    tinv = [eye + x0[h] for h in heads]
    x16 = b16(x0)
    p = mm(x16, x16)
    p16 = b16(p)
    tinv = [t + d for t, d in zip(tinv, mm(b16(tinv), p16))]
    p16 = b16(mm(p16, p16))
    tinv = [t + d for t, d in zip(tinv, mm(b16(tinv), p16))]
    s = SUBLANES
    while s < C:
        m = jnp.logical_and(blk(2 * s), jnp.logical_not(blk(s)))
        off16 = b16([jnp.where(m, bm[h], 0.0) for h in heads])
        t16 = b16(tinv)
        tinv = [t - d for t, d in zip(tinv, mm(b16(mm(t16, off16)), t16))]
        s *= 2
    t16 = b16(tinv)

    v = [v_ref[:, hs[h]] for h in heads]
    sol = mm(t16, [jnp.concatenate([(v[h] * bcol[h]).astype(BF16), (kb[h] * eg[h]).astype(BF16)], axis=1)
                   for h in heads])
    value = [x[:, :HEAD_DIM] for x in sol]
    kcd16 = b16([x[:, HEAD_DIM:] for x in sol])
    st = [s_ref[0, h] for h in heads]
    st16 = b16(st)
    u16 = b16([value[h] - d for h, d in zip(heads, mm(kcd16, st16))])
    o = [a + c for a, c in zip(mm(b16([q[h] * eg[h] for h in heads]), st16), mm(attn16, u16))]
    glast = [gcol[h][C - 1:C, :] for h in heads]
    kd16 = b16([k[h] * jnp.exp(glast[h] - gcol[h]) for h in heads])
    ku = [lax.dot_general(x, y, tn, preferred_element_type=F32) for x, y in zip(kd16, u16)]
    for h in heads:
        s_ref[0, h] = st[h] * jnp.exp(glast[h]) + ku[h]
        on = o[h] * lax.rsqrt(jnp.mean(o[h] * o[h], axis=-1, keepdims=True) + RMS_EPS) * gon
        o_ref[:, hs[h]] = (on * _silu(z_ref[:, hs[h]])).astype(o_ref.dtype)


def _gdn_scan_prompt(qkv, proj, z_col0, beta, gcum, g_onorm, o_sample, Bp, T, H):
    Mp = Bp * T
    Bs = o_sample.shape[0]
    hb = _pick_tile(H, 8, 8) if H % 8 == 0 else H
    tt = CHUNK
    assert T % CHUNK == 0 and (hb % 8 == 0 or hb == H) and Bs <= tt
    nhg, ntt = H // hb, T // tt
    n_chunks = Bp * ntt
    cw = hb * HEAD_DIM
    assert z_col0 % cw == 0
    beta_g = beta.reshape(Mp, nhg, hb).transpose(1, 0, 2)
    gcol_g = gcum.reshape(Mp, nhg, hb).transpose(1, 0, 2)
    grow_g = gcum.reshape(Mp // CHUNK, CHUNK, H).transpose(0, 2, 1)
    body = functools.partial(_gdn_scan_body, hb=hb, chunks_per_seq=ntt, n_chunks=n_chunks)
    row = lambda s: jnp.minimum(s, n_chunks - 1)
    blk = (4 * _nbytes((tt, cw), F32) + 2 * _nbytes((tt, LANES), F32) + _nbytes((hb, LANES), F32)
           + _nbytes((tt, cw), BF16) + _nbytes((Bs, cw), BF16) + _nbytes((hb, HEAD_DIM, HEAD_DIM), F32))
    temporaries = 8 * hb * _nbytes((CHUNK, CHUNK), F32)
    return pl.pallas_call(
        body,
        grid=(nhg, n_chunks + 1),
        in_specs=[
            pl.BlockSpec((tt, cw), lambda h, s: (row(s), h)),
            pl.BlockSpec((tt, cw), lambda h, s: (row(s), nhg + h)),
            pl.BlockSpec((tt, cw), lambda h, s: (row(s), 2 * nhg + h)),
            pl.BlockSpec((tt, cw), lambda h, s: (row(s), z_col0 // cw + h)),
            pl.BlockSpec((1, tt, hb), lambda h, s: (h, row(s), 0)),
            pl.BlockSpec((1, tt, hb), lambda h, s: (h, row(s), 0)),
            pl.BlockSpec((1, hb, CHUNK), lambda h, s: (row(s), h, 0)),
            pl.BlockSpec((1, HEAD_DIM), lambda h, s: (0, 0)),
            pl.BlockSpec((Bs, cw), lambda h, s: (0, h)),
        ],
        out_specs=[
            pl.BlockSpec((tt, cw), lambda h, s: (s, h)),
            pl.BlockSpec((1, hb, HEAD_DIM, HEAD_DIM), lambda h, s: (row(s) // ntt, h, 0, 0)),
        ],
        out_shape=[jax.ShapeDtypeStruct((Mp + Bs, H * HEAD_DIM), BF16),
                   jax.ShapeDtypeStruct((Bp, H, HEAD_DIM, HEAD_DIM), F32)],
        compiler_params=_params(("parallel", "arbitrary"), blk, temporaries),
        name="gdn_scan_prompt",
    )(qkv, qkv, qkv, proj, beta_g, gcol_g, grow_g, g_onorm.reshape(1, HEAD_DIM), o_sample)


def _gdn_step_body(a_ref, beta_ref, s_ref, kt_ref, qt_ref, v_ref, z_ref, gon_ref, so_ref, o_ref, orow_ref, *, n_heads):
    b = pl.program_id(0)
    for h in range(n_heads):
        st = s_ref[0, h]
        a = a_ref[b, h]
        bt = beta_ref[b, h]
        kc = jnp.broadcast_to(kt_ref[0, :, h:h + 1], st.shape)
        qc = jnp.broadcast_to(qt_ref[0, :, h:h + 1], st.shape)
        sk = jnp.sum(st * kc, axis=0, keepdims=True)
        w = v_ref[0, h:h + 1, :] - a * sk
        sn = a * st + (bt * kc) * w
        so_ref[0, h] = sn
        orow_ref[h:h + 1, :] = jnp.sum(sn * qc, axis=0, keepdims=True)
    o = orow_ref[...]
    on = o * lax.rsqrt(jnp.mean(o * o, axis=-1, keepdims=True) + RMS_EPS) * gon_ref[...]
    o_ref[0] = (on * _silu(z_ref[0])).astype(o_ref.dtype)


def _gdn_step_sample(qkv_s, z_s, beta_s, a_s, state, g_onorm):
    Bs, H = beta_s.shape
    hd = H * HEAD_DIM
    q3 = qkv_s[:, :hd].reshape(Bs, H, HEAD_DIM).transpose(0, 2, 1)
    k3 = qkv_s[:, hd:2 * hd].reshape(Bs, H, HEAD_DIM).transpose(0, 2, 1)
    v3 = qkv_s[:, 2 * hd:].reshape(Bs, H, HEAD_DIM)
    z3 = z_s.reshape(Bs, H, HEAD_DIM)
    smem = pl.BlockSpec(memory_space=pltpu.SMEM)
    body = functools.partial(_gdn_step_body, n_heads=H)
    blk = 2 * _nbytes((H, HEAD_DIM, HEAD_DIM), F32) + 6 * _nbytes((HEAD_DIM, LANES), F32)
    so, o = pl.pallas_call(
        body,
        grid=(Bs,),
        in_specs=[smem, smem,
                  pl.BlockSpec((1, H, HEAD_DIM, HEAD_DIM), lambda b: (b, 0, 0, 0)),
                  pl.BlockSpec((1, HEAD_DIM, H), lambda b: (b, 0, 0)),
                  pl.BlockSpec((1, HEAD_DIM, H), lambda b: (b, 0, 0)),
                  pl.BlockSpec((1, H, HEAD_DIM), lambda b: (b, 0, 0)),
                  pl.BlockSpec((1, H, HEAD_DIM), lambda b: (b, 0, 0)),
                  pl.BlockSpec((1, HEAD_DIM), lambda b: (0, 0))],
        out_specs=[pl.BlockSpec((1, H, HEAD_DIM, HEAD_DIM), lambda b: (b, 0, 0, 0)),
                   pl.BlockSpec((1, H, HEAD_DIM), lambda b: (b, 0, 0))],
        out_shape=[jax.ShapeDtypeStruct((Bs, H, HEAD_DIM, HEAD_DIM), F32),
                   jax.ShapeDtypeStruct((Bs, H, HEAD_DIM), BF16)],
        scratch_shapes=[pltpu.VMEM((H, HEAD_DIM), F32)],
        compiler_params=_params(("arbitrary",), blk),
        name="gdn_step_sample",
    )(a_s, beta_s, state, k3, q3, v3, z3, g_onorm.reshape(1, HEAD_DIM))
    return o.reshape(Bs, hd), so


def _seq_tails(a, Bp, T, n, ncols):
    return jnp.stack([lax.slice(a, (b * T + T - n, 0), ((b + 1) * T, ncols)) for b in range(Bp)])


def _wcol_bytes(*ws):
    return sum(w.shape[1] * w.dtype.itemsize for w in ws)


def _conformer_layer(h, xn, g_next, Mp, Bp, T, Bs, state, j, w_pw1, w_dw, b_dw, ln_g, ln_b, w_pw2):
    M, D = h.shape
    Dc = w_pw1.shape[2] // 2
    tm, tn = _mm_tiles(M, D, Dc, 2 * _wcol_bytes(w_pw1))
    glu = _matmul(_mm_glu_body, "mm_glu", M, Dc, tm, tn,
                  _xn_specs(xn, tm) + [_w_spec(w_pw1, j, tn), _w_spec(w_pw1, j, tn, Dc // tn)], F32)
    c_s, new_s = _conf_conv_sample(glu, Mp, Bs, jnp.swapaxes(state, 0, 1), w_dw, b_dw, ln_g, ln_b)
    new_s = jnp.swapaxes(new_s, 0, 1)
    c = _conf_conv_prompt(glu, Bp, T, c_s, w_dw, b_dw, ln_g, ln_b)
    tm, tn = _mm_tiles(M, Dc, D, _wcol_bytes(w_pw2))
    h, xn = _matmul(_mm_res_body, "mm_pw2", M, D, tm, tn,
                    [_x_spec(c, tm), _w_spec(w_pw2, j, tn), _r_spec(h, tm, tn), _g_spec(g_next, tn)], F32, True)
    new_p = _seq_tails(glu, Bp, T, w_dw.shape[0] - 1, Dc)
    return h, xn, new_p, new_s


def _gdn_layer(h, xn, g_next, Mp, Bp, T, Bs, conv_state, gdn_state, j, w_in, w_conv, a_log, dt_bias, g_onorm, w_out):
    M, D = h.shape
    H = a_log.shape[0]
    hd = H * HEAD_DIM
    qkv_dim = 3 * hd
    n_main = qkv_dim + hd
    tm, tn = _mm_tiles(M, D, n_main, _wcol_bytes(w_in))
    w_in_t = jnp.swapaxes(w_in, 1, 2)
    wt_spec = lambda rows, row_block0: (w_in_t, (None, rows, D), lambda i, c: (j, c + row_block0, 0))
    proj = _matmul(_mm_plain_wt_body, "mm_gdn_in", M, n_main, tm, tn, _xn_specs(xn, tm) + [wt_spec(tn, 0)], F32)
    assert n_main % (2 * H) == 0
    logits = _matmul(_mm_plain_wt_body, "mm_gdn_gates", M, 2 * H, tm, 2 * H,
                     _xn_specs(xn, tm) + [wt_spec(2 * H, n_main // (2 * H))], F32)

    beta_p, gcum_p = _gates(logits, a_log, dt_bias, 0, Mp, CHUNK)
    beta_s, a_s = _gates(logits, a_log, dt_bias, Mp, Bs, 1)

    qkv_s, new_conv_s = _qkv_prep_sample(proj, Mp, Bs, qkv_dim, jnp.swapaxes(conv_state, 0, 1), w_conv)
    new_conv_s = jnp.swapaxes(new_conv_s, 0, 1)
    o_s, s_s = _gdn_step_sample(qkv_s, lax.slice(proj, (Mp, qkv_dim), (M, n_main)), beta_s, a_s, gdn_state, g_onorm)

    qkv_p = _qkv_prep_prompt(proj, Bp, T, qkv_dim, w_conv)
    o, s_p = _gdn_scan_prompt(qkv_p, proj, qkv_dim, beta_p, gcum_p, g_onorm, o_s, Bp, T, H)
    tm, tn = _mm_tiles(M, hd, D, _wcol_bytes(w_out))
    h, xn = _matmul(_mm_res_body, "mm_gdn_out", M, D, tm, tn,
                    [_x_spec(o, tm), _w_spec(w_out, j, tn), _r_spec(h, tm, tn), _g_spec(g_next, tn)], F32, True)
    new_conv_p = _seq_tails(proj, Bp, T, w_conv.shape[0] - 1, qkv_dim)
    return h, xn, new_conv_p, new_conv_s, s_p, s_s


def _ffn_ple(h, xn, p_all, i, w_gate, w_up, w_down, g_ple, w_pg, w_pp, g_next):
    M, D = h.shape
    F = w_gate.shape[2]
    tm, tn = _mm_tiles(M, D, F, _wcol_bytes(w_gate, w_up))
    act = _matmul(_mm_swiglu_body, "mm_ffn_up", M, F, tm, tn,
                  _xn_specs(xn, tm) + [_w_spec(w_gate, i, tn), _w_spec(w_up, i, tn)], BF16)
    tm, tn = _mm_tiles(M, F, D, _wcol_bytes(w_down))
    h, xn = _matmul(_mm_res_body, "mm_ffn_down", M, D, tm, tn,
                    [_x_spec(act, tm), _w_spec(w_down, i, tn), _r_spec(h, tm, tn), _g_spec(g_ple, tn)], F32, True)
    tm, tn = _mm_tiles(M, D, D, _wcol_bytes(w_pg, w_pp))
    p_spec = (p_all, (None, tm, p_all.shape[2]), lambda r, c: (i, r, 0))
    operands = _xn_specs(xn, tm) + [p_spec, _w_spec(w_pg, i, tn), _w_spec(w_pp, i, tn), _r_spec(h, tm, tn)]
    if g_next is None:
        return _matmul(_mm_ple_last_body, "mm_ple", M, D, tm, tn, operands, F32), None
    return _matmul(_mm_ple_body, "mm_ple", M, D, tm, tn, operands + [_g_spec(g_next, tn)], F32, True)


def kernel(x_prompt, x_sample, p_prompt, p_sample, state_conv_conformer, state_conv_qkv, state_gdn, g_mix, g_ffn, g_ple, g_final, conf_w_pw1, conf_w_dw, conf_b_dw, conf_ln_g, conf_ln_b, conf_w_pw2, gdn_w_in, gdn_w_conv, gdn_a_log, gdn_dt_bias, gdn_g_onorm, gdn_w_out, ffn_w_gate, ffn_w_up, ffn_w_down, ple_w_gate, ple_w_proj):
    Bp, T, D = x_prompt.shape
    Bs, Ts, _ = x_sample.shape
    assert Ts == 1
    depth = g_mix.shape[0]
    Mp = Bp * T
    M = Mp + Bs
    h, xn = _embed(x_prompt.reshape(Mp, D), x_sample.reshape(Bs, D), g_mix[0])
    p_all = jnp.concatenate([p_prompt.reshape(depth, Mp, -1), p_sample.reshape(depth, Bs, -1)], axis=1).astype(BF16)
    ffn_w_down16 = ffn_w_down.astype(BF16)

    conf_p, conf_s, qkv_p, qkv_s, gdn_p, gdn_s = [], [], [], [], [], []
    for i in range(depth):
        j = i // 2
        if i % 2 == 0:
            h, xn, np_, ns_ = _conformer_layer(h, xn, g_ffn[i], Mp, Bp, T, Bs, state_conv_conformer[j], j, conf_w_pw1,
                                               conf_w_dw[j], conf_b_dw[j], conf_ln_g[j], conf_ln_b[j], conf_w_pw2)
            conf_p.append(np_)
            conf_s.append(ns_)
        else:
            h, xn, cp_, cs_, sp_, ss_ = _gdn_layer(h, xn, g_ffn[i], Mp, Bp, T, Bs, state_conv_qkv[j], state_gdn[j], j,
                                                   gdn_w_in, gdn_w_conv[j], gdn_a_log[j], gdn_dt_bias[j],
                                                   gdn_g_onorm[j], gdn_w_out)
            qkv_p.append(cp_)
            qkv_s.append(cs_)
            gdn_p.append(sp_)
            gdn_s.append(ss_)
        h, xn = _ffn_ple(h, xn, p_all, i, ffn_w_gate, ffn_w_up, ffn_w_down16, g_ple[i], ple_w_gate, ple_w_proj,
                         g_mix[i + 1] if i + 1 < depth else None)

    y_p = _rmsnorm(h, g_final, F32, 0, Mp).reshape(Bp, T, D)
    y_s = _rmsnorm(h, g_final, F32, Mp, Bs).reshape(Bs, 1, D)
    return (y_p, y_s, jnp.stack(conf_p), jnp.stack(qkv_p), jnp.stack(gdn_p),
            jnp.stack(conf_s), jnp.stack(qkv_s), jnp.stack(gdn_s))
```

```python
import functools

import jax
import jax.numpy as jnp
from jax import lax
from jax.experimental import pallas as pl
from jax.experimental.pallas import tpu as pltpu

F32 = jnp.float32
BF16 = jnp.bfloat16

RMS_EPS = 1e-6
LN_EPS = 1e-5
L2_EPS = 1e-6
CHUNK = 256
HEAD_DIM = 128
LANES = 128
SUBLANES = 8
V7X_VMEM_LIMIT_CAP = 60000 * 1024
VMEM_INTERNAL_SCRATCH = 8 * 2**20


def _pick_tile(n, cap, mult):
    best = None
    for d in range(mult, min(n, cap) + 1, mult):
        if n % d == 0:
            best = d
    return n if best is None else best


def _params(sem, block_bytes, scratch_bytes=0):
    need = 2 * block_bytes + scratch_bytes + VMEM_INTERNAL_SCRATCH
    return pltpu.CompilerParams(dimension_semantics=sem,
                                vmem_limit_bytes=int(min(max(need, 32 * 2**20), V7X_VMEM_LIMIT_CAP)))


def _nbytes(shape, dtype):
    n = 1
    for s in shape:
        n *= 1 if s is None else s
    return n * jnp.dtype(dtype).itemsize


def _bdot(a, b):
    return jnp.dot(a.astype(BF16), b.astype(BF16), preferred_element_type=F32)


def _silu(x):
    return x * jax.nn.sigmoid(x)


def _rms_body(x_ref, g_ref, o_ref):
    x = x_ref[...]
    y = x * lax.rsqrt(jnp.mean(x * x, axis=-1, keepdims=True) + RMS_EPS)
    o_ref[...] = (y * g_ref[...]).astype(o_ref.dtype)


def _rmsnorm(x, g, out_dtype, row0=0, rows=None):
    M, D = x.shape
    rows = M if rows is None else rows
    tm = _pick_tile(rows, 320, 16)
    assert row0 % tm == 0
    blk = _nbytes((tm, D), F32) + _nbytes((tm, D), out_dtype)
    return pl.pallas_call(
        _rms_body,
        grid=(rows // tm,),
        in_specs=[pl.BlockSpec((tm, D), lambda i: (i + row0 // tm, 0)),
                  pl.BlockSpec((1, D), lambda i: (0, 0))],
        out_specs=pl.BlockSpec((tm, D), lambda i: (i, 0)),
        out_shape=jax.ShapeDtypeStruct((rows, D), out_dtype),
        compiler_params=_params(("parallel",), blk),
        name="rmsnorm",
    )(x, g.reshape(1, D))


def _embed_body(xp_ref, xs_ref, g_ref, h_ref, xg_ref, ss_ref, *, n_prompt_tiles):
    def emit(x):
        h_ref[...] = x
        xg_ref[...] = (x * g_ref[...]).astype(xg_ref.dtype)
        ss_ref[...] = jnp.sum(x * x, axis=-1, keepdims=True)

    @pl.when(pl.program_id(0) < n_prompt_tiles)
    def _():
        emit(xp_ref[...])

    @pl.when(pl.program_id(0) >= n_prompt_tiles)
    def _():
        emit(xs_ref[...])


def _embed(xp, xs, g):
    (Mp, D), Bs = xp.shape, xs.shape[0]
    tr = Bs
    assert Mp % tr == 0 and tr % SUBLANES == 0
    npt = Mp // tr
    body = functools.partial(_embed_body, n_prompt_tiles=npt)
    blk = 3 * _nbytes((tr, D), F32) + _nbytes((tr, D), BF16) + _nbytes((tr, LANES), F32)
    h, xg, ss = pl.pallas_call(
        body,
        grid=(npt + 1,),
        in_specs=[pl.BlockSpec((tr, D), lambda i: (jnp.minimum(i, npt - 1), 0)),
                  pl.BlockSpec((tr, D), lambda i: (0, 0)),
                  pl.BlockSpec((1, D), lambda i: (0, 0))],
        out_specs=[pl.BlockSpec((tr, D), lambda i: (i, 0)), pl.BlockSpec((tr, D), lambda i: (i, 0)),
                   pl.BlockSpec((tr, 1), lambda i: (i, 0))],
        out_shape=[jax.ShapeDtypeStruct((Mp + Bs, D), F32), jax.ShapeDtypeStruct((Mp + Bs, D), BF16),
                   jax.ShapeDtypeStruct((Mp + Bs, 1), F32)],
        compiler_params=_params(("parallel",), blk),
        name="embed_norm",
    )(xp, xs, g.reshape(1, D))
    return h, (xg, ss)


def _wdot(x_ref, w_ref):
    return jnp.dot(x_ref[...], w_ref[...].astype(BF16), preferred_element_type=F32)


def _row_scale(ss_ref, width):
    return lax.rsqrt(ss_ref[...] * (1.0 / width) + RMS_EPS)


def _emit_normalised(val, g_ref, o_ref, xg_ref, ss_ref):
    o_ref[...] = val
    xg_ref[...] = (val * g_ref[...]).astype(xg_ref.dtype)
    part = jnp.sum(val * val, axis=-1, keepdims=True)
    j = pl.program_id(1)

    @pl.when(j == 0)
    def _():
        ss_ref[...] = part

    @pl.when(j > 0)
    def _():
        ss_ref[...] += part


def _mm_glu_body(x_ref, ss_ref, wa_ref, wg_ref, o_ref):
    r = _row_scale(ss_ref, x_ref.shape[1])
    o_ref[...] = ((_wdot(x_ref, wa_ref) * r) * jax.nn.sigmoid(_wdot(x_ref, wg_ref) * r)).astype(o_ref.dtype)


def _mm_swiglu_body(x_ref, ss_ref, wg_ref, wu_ref, o_ref):
    r = _row_scale(ss_ref, x_ref.shape[1])
    o_ref[...] = (_silu(_wdot(x_ref, wg_ref) * r) * (_wdot(x_ref, wu_ref) * r)).astype(o_ref.dtype)


def _mm_swiglu_cast_body(x_ref, ss_ref, wg_ref, wu_ref, wd_ref, o_ref, wd16_ref):
    _mm_swiglu_body(x_ref, ss_ref, wg_ref, wu_ref, o_ref)
    wd16_ref[...] = wd_ref[...].astype(wd16_ref.dtype)


def _mm_plain_body(x_ref, ss_ref, w_ref, o_ref):
    o_ref[...] = (_wdot(x_ref, w_ref) * _row_scale(ss_ref, x_ref.shape[1])).astype(o_ref.dtype)


def _mm_plain_wt_body(x_ref, ss_ref, wt_ref, o_ref):
    d = lax.dot_general(x_ref[...], wt_ref[...].astype(BF16), (((1,), (1,)), ((), ())), preferred_element_type=F32)
    o_ref[...] = (d * _row_scale(ss_ref, x_ref.shape[1])).astype(o_ref.dtype)


def _mm_res_body(x_ref, w_ref, r_ref, g_ref, o_ref, xg_ref, sso_ref):
    _emit_normalised(r_ref[...] + _wdot(x_ref, w_ref), g_ref, o_ref, xg_ref, sso_ref)


def _ple_value(x_ref, ss_ref, p_ref, wg_ref, wp_ref, r_ref):
    gate = jax.nn.sigmoid(_wdot(x_ref, wg_ref) * _row_scale(ss_ref, x_ref.shape[1]))
    return r_ref[...] + gate * _wdot(p_ref, wp_ref)


def _mm_ple_body(x_ref, ss_ref, p_ref, wg_ref, wp_ref, r_ref, g_ref, o_ref, xg_ref, sso_ref):
    _emit_normalised(_ple_value(x_ref, ss_ref, p_ref, wg_ref, wp_ref, r_ref), g_ref, o_ref, xg_ref, sso_ref)


def _mm_ple_last_body(x_ref, ss_ref, p_ref, wg_ref, wp_ref, r_ref, o_ref):
    o_ref[...] = _ple_value(x_ref, ss_ref, p_ref, wg_ref, wp_ref, r_ref)


def _matmul(body, name, M, N, tm, tn, operands, out_dtype, emit_normalised=False, side_out=None):
    blk = sum(_nbytes(bs, a.dtype) for a, bs, _ in operands) + _nbytes((tm, tn), out_dtype)
    out_specs = [pl.BlockSpec((tm, tn), lambda i, j: (i, j))]
    out_shape = [jax.ShapeDtypeStruct((M, N), out_dtype)]
    if emit_normalised:
        out_specs += [pl.BlockSpec((tm, tn), lambda i, j: (i, j)), pl.BlockSpec((tm, 1), lambda i, j: (i, 0))]
        out_shape += [jax.ShapeDtypeStruct((M, N), BF16), jax.ShapeDtypeStruct((M, 1), F32)]
        blk += _nbytes((tm, tn), BF16) + _nbytes((tm, LANES), F32)
    if side_out is not None:
        out_specs.append(pl.BlockSpec(side_out[2], side_out[3]))
        out_shape.append(jax.ShapeDtypeStruct(side_out[0], side_out[1]))
        blk += _nbytes(side_out[2], side_out[1])
    outs = pl.pallas_call(
        body,
        grid=(M // tm, N // tn),
        in_specs=[pl.BlockSpec(bs, im) for _, bs, im in operands],
        out_specs=out_specs,
        out_shape=out_shape,
        compiler_params=_params(("parallel", "arbitrary"), blk),
        name=name,
    )(*[a for a, _, _ in operands])
    main = (outs[0], (outs[1], outs[2])) if emit_normalised else outs[0]
    return main if side_out is None else (main, outs[-1])


def _x_spec(x, tm):
    return (x, (tm, x.shape[1]), lambda i, j: (i, 0))


def _xn_specs(xn, tm):
    xg, ss = xn
    return [_x_spec(xg, tm), (ss, (tm, 1), lambda i, j: (i, 0))]


def _w_spec(w, layer, tn, col_block0=0):
    return (w, (None, w.shape[1], tn), lambda i, j: (layer, 0, j + col_block0))


def _r_spec(r, tm, tn):
    return (r, (tm, tn), lambda i, j: (i, j))


def _g_spec(g, tn):
    return (g.reshape(1, -1), (1, tn), lambda i, j: (0, j))


def _mm_tiles(M, K, N, w_bytes_per_col):
    tn = _pick_tile(N, 512 if 512 * w_bytes_per_col <= 9 * 2**20 else 256, LANES)
    tm_cap = 1040 if K <= 4096 else 640
    return _pick_tile(M, tm_cap, 16), tn


def _conv_blocks(load_blk, w_row, n_taps, n_out, width):
    qmax = (n_taps - 1) // SUBLANES
    rows = lax.broadcasted_iota(jnp.int32, (SUBLANES, width), 0)
    x = {k: load_blk(k) for k in range(-(qmax + 1), n_out)}
    acc = [None] * n_out
    for r in range(min(SUBLANES, n_taps)):
        if r == 0:
            y = x
        else:
            rot = {k: pltpu.roll(v, r, 0) for k, v in x.items()}
            y = {k: jnp.where(rows >= r, rot[k], rot[k - 1]) for k in range(-qmax, n_out)}
        for q in range(qmax + 1):
            d = SUBLANES * q + r
            if d >= n_taps:
                continue
            w = w_row(d)
            for j in range(n_out):
                t = y[j - q] * w
                acc[j] = t if acc[j] is None else acc[j] + t
    return acc


CONF_HALO = 32
CONV_STRIP = 8
CONV_ROWS_CAP = 256
QKV_ROWS_CAP = 512


def _conf_conv_body(x_ref, halo_ref, cs_ref, w_ref, b_ref, lg_ref, lb_ref, o_ref, xe_ref, y_ref, *,
                    tb, ncb, n_taps, tiles_per_seq, n_tiles):
    step = pl.program_id(0)

    @pl.when(step == n_tiles)
    def _():
        o_ref[0:cs_ref.shape[0], :] = cs_ref[...].astype(o_ref.dtype)

    @pl.when(step < n_tiles)
    def _():
        _conf_conv_tile(x_ref, halo_ref, w_ref, b_ref, lg_ref, lb_ref, o_ref, xe_ref, y_ref,
                        first=step % tiles_per_seq == 0, tb=tb, ncb=ncb, n_taps=n_taps)


def _conf_conv_tile(x_ref, halo_ref, w_ref, b_ref, lg_ref, lb_ref, o_ref, xe_ref, y_ref, *, first, tb, ncb, n_taps):
    for cb in range(ncb):
        sl = slice(cb * LANES, (cb + 1) * LANES)
        xe_ref[cb, CONF_HALO:CONF_HALO + tb, :] = x_ref[:, sl]
        xe_ref[cb, 0:CONF_HALO, :] = jnp.where(first, 0.0, halo_ref[:, sl])

    strip_rows = SUBLANES * CONV_STRIP
    n_strips = tb // strip_rows

    def col_loop(cb, carry):
        def strip_loop(s, carry2):
            r0 = pl.multiple_of(s * strip_rows, strip_rows)
            acc = _conv_blocks(
                lambda k: xe_ref[cb, pl.ds(r0 + (CONF_HALO + SUBLANES * k), SUBLANES), :],
                lambda d: w_ref[cb, pl.ds(n_taps - 1 - d, 1), :],
                n_taps, CONV_STRIP, LANES)
            bias = b_ref[cb]
            for j in range(CONV_STRIP):
                y_ref[cb, pl.ds(r0 + SUBLANES * j, SUBLANES), :] = acc[j] + bias
            return carry2
        return lax.fori_loop(0, n_strips, strip_loop, carry)

    lax.fori_loop(0, ncb, col_loop, 0)

    ln_rows = 32
    inv_c = 1.0 / (ncb * LANES)

    def ln_loop(s, carry):
        r0 = pl.multiple_of(s * ln_rows, ln_rows)
        y = y_ref[:, pl.ds(r0, ln_rows), :]
        mu = jnp.sum(jnp.sum(y, axis=0), axis=-1, keepdims=True) * inv_c
        yc = y - mu[None]
        var = jnp.sum(jnp.sum(yc * yc, axis=0), axis=-1, keepdims=True) * inv_c
        rstd = lax.rsqrt(var + LN_EPS)
        for cb in range(ncb):
            t = yc[cb] * rstd * lg_ref[cb] + lb_ref[cb]
            o_ref[pl.ds(r0, ln_rows), cb * LANES:(cb + 1) * LANES] = _silu(t).astype(o_ref.dtype)
        return carry

    lax.fori_loop(0, tb // ln_rows, ln_loop, 0)


def _col_major(v, pad_rows=None):
    R, C = v.shape
    out = v.reshape(R, C // LANES, LANES).transpose(1, 0, 2)
    if pad_rows is not None and pad_rows > R:
        out = jnp.pad(out, ((0, 0), (0, pad_rows - R), (0, 0)))
    return out


def _conf_conv_prompt(glu, Bp, T, c_sample, w_dw, b_dw, ln_g, ln_b):
    Dc = glu.shape[1]
    Bs = c_sample.shape[0]
    n_taps = w_dw.shape[0]
    assert n_taps - 1 <= CONF_HALO and T >= CONF_HALO
    tb = _pick_tile(T, CONV_ROWS_CAP, SUBLANES * CONV_STRIP)
    assert Bs <= tb
    ncb = Dc // LANES
    n_tiles = Bp * T // tb
    body = functools.partial(_conf_conv_body, tb=tb, ncb=ncb, n_taps=n_taps, tiles_per_seq=T // tb, n_tiles=n_tiles)
    hb = tb // CONF_HALO
    blk = (_nbytes((tb, Dc), F32) + _nbytes((CONF_HALO, Dc), F32) + _nbytes((tb, Dc), BF16) + _nbytes((Bs, Dc), F32)
           + _nbytes((ncb, 32 + 3, LANES), F32))
    scratch = _nbytes((ncb, tb + CONF_HALO, LANES), F32) + _nbytes((ncb, tb, LANES), F32)
    tile = lambda s: jnp.minimum(s, n_tiles - 1)
    const3 = lambda s: (0, 0, 0)
    return pl.pallas_call(
        body,
        grid=(n_tiles + 1,),
        in_specs=[
            pl.BlockSpec((tb, Dc), lambda s: (tile(s), 0)),
            pl.BlockSpec((CONF_HALO, Dc), lambda s: (jnp.maximum(tile(s) * hb - 1, 0), 0)),
            pl.BlockSpec((Bs, Dc), lambda s: (0, 0)),
            pl.BlockSpec((ncb, 32, LANES), const3),
            pl.BlockSpec((ncb, 1, LANES), const3),
            pl.BlockSpec((ncb, 1, LANES), const3),
            pl.BlockSpec((ncb, 1, LANES), const3),
        ],
        out_specs=pl.BlockSpec((tb, Dc), lambda s: (s, 0)),
        out_shape=jax.ShapeDtypeStruct((Bp * T + Bs, Dc), BF16),
        scratch_shapes=[pltpu.VMEM((ncb, tb + CONF_HALO, LANES), F32), pltpu.VMEM((ncb, tb, LANES), F32)],
        compiler_params=_params(("arbitrary",), blk, scratch),
        name="conf_conv_prompt",
    )(glu, glu, c_sample, _col_major(w_dw, 32), _col_major(b_dw[None]), _col_major(ln_g[None]), _col_major(ln_b[None]))


def _conf_conv_sample_body(st_ref, x_ref, w_ref, b_ref, lg_ref, lb_ref, o_ref, so_ref, acc_ref, *, n_prev):
    w = pl.program_id(0)

    @pl.when(w == 0)
    def _():
        acc_ref[...] = x_ref[...] * w_ref[pl.ds(n_prev, 1), :]

    @pl.when(w < n_prev)
    def _():
        st = st_ref[...]
        acc_ref[...] += st * w_ref[pl.ds(w, 1), :]
        so_ref[...] = st

    @pl.when(w == n_prev)
    def _():
        so_ref[...] = x_ref[...]
        y = acc_ref[...] + b_ref[...]
        mu = jnp.mean(y, axis=-1, keepdims=True)
        yc = y - mu
        rstd = lax.rsqrt(jnp.mean(yc * yc, axis=-1, keepdims=True) + LN_EPS)
        o_ref[...] = _silu(yc * rstd * lg_ref[...] + lb_ref[...])


def _conf_conv_sample(glu, Mp, Bs, state_t, w_dw, b_dw, ln_g, ln_b):
    Dc = glu.shape[1]
    n_taps = w_dw.shape[0]
    n_prev = n_taps - 1
    assert Mp % Bs == 0
    blk = 4 * _nbytes((Bs, Dc), F32) + _nbytes((n_taps + 3 * SUBLANES, Dc), F32)
    row = lambda w: (0, 0)
    body = functools.partial(_conf_conv_sample_body, n_prev=n_prev)
    return pl.pallas_call(
        body,
        grid=(n_prev + 1,),
        in_specs=[
            pl.BlockSpec((None, Bs, Dc), lambda w: (jnp.minimum(w, n_prev - 1), 0, 0)),
            pl.BlockSpec((Bs, Dc), lambda w: (Mp // Bs, 0)),
            pl.BlockSpec((n_taps, Dc), row),
            pl.BlockSpec((1, Dc), row), pl.BlockSpec((1, Dc), row), pl.BlockSpec((1, Dc), row),
        ],
        out_specs=[pl.BlockSpec((Bs, Dc), lambda w: (0, 0)),
                   pl.BlockSpec((None, Bs, Dc), lambda w: (jnp.maximum(w - 1, 0), 0, 0))],
        out_shape=[jax.ShapeDtypeStruct((Bs, Dc), F32), jax.ShapeDtypeStruct((n_prev, Bs, Dc), F32)],
        scratch_shapes=[pltpu.VMEM((Bs, Dc), F32)],
        compiler_params=_params(("arbitrary",), blk, _nbytes((Bs, Dc), F32)),
        name="conf_conv_sample",
    )(state_t, glu, w_dw, b_dw[None], ln_g[None], ln_b[None])


def _qkv_finish(y, kind):
    y = _silu(y)
    if kind == 2:
        return y
    fac = lax.rsqrt(jnp.sum(y * y, axis=-1, keepdims=True) + L2_EPS)
    return y * (fac * HEAD_DIM ** -0.5 if kind == 0 else fac)


def _for_each_kind(kind, fn):
    for k in range(3):
        pl.when(kind == k)(functools.partial(fn, k))


QKV_HALO = 8


def _qkv_prep_body(x_ref, halo_ref, w_ref, o_ref, xe_ref, *, tb, cw, n_taps, groups_per_kind):
    i = pl.program_id(1)
    kind = pl.program_id(2) // groups_per_kind
    xe_ref[QKV_HALO:QKV_HALO + tb, :] = x_ref[...]
    xe_ref[0:QKV_HALO, :] = jnp.where(i == 0, 0.0, halo_ref[...])
    strip_rows = SUBLANES * CONV_STRIP

    def run(static_kind):
        def strip_loop(s, carry):
            r0 = pl.multiple_of(s * strip_rows, strip_rows)
            for cb in range(cw // LANES):
                sl = slice(cb * LANES, (cb + 1) * LANES)
                acc = _conv_blocks(
                    lambda k: xe_ref[pl.ds(r0 + (QKV_HALO + SUBLANES * k), SUBLANES), sl],
                    lambda d: w_ref[pl.ds(n_taps - 1 - d, 1), sl],
                    n_taps, CONV_STRIP, LANES)
                for j in range(CONV_STRIP):
                    o_ref[pl.ds(r0 + SUBLANES * j, SUBLANES), sl] = _qkv_finish(acc[j], static_kind)
            return carry

        lax.fori_loop(0, tb // strip_rows, strip_loop, 0)

    _for_each_kind(kind, run)


def _qkv_prep_prompt(proj, Bp, T, qkv_dim, w_conv):
    n_taps = w_conv.shape[0]
    assert n_taps - 1 <= QKV_HALO
    tb = _pick_tile(T, QKV_ROWS_CAP, SUBLANES * CONV_STRIP)
    cw = _pick_tile(qkv_dim // 3, 1024, LANES)
    nt = T // tb
    ncg = qkv_dim // cw
    body = functools.partial(_qkv_prep_body, tb=tb, cw=cw, n_taps=n_taps, groups_per_kind=ncg // 3)
    hb = tb // QKV_HALO
    blk = 2 * _nbytes((tb, cw), F32) + 2 * _nbytes((QKV_HALO, cw), F32)
    return pl.pallas_call(
        body,
        grid=(Bp, nt, ncg),
        in_specs=[
            pl.BlockSpec((tb, cw), lambda b, i, c: (b * nt + i, c)),
            pl.BlockSpec((QKV_HALO, cw), lambda b, i, c: (jnp.maximum((b * nt + i) * hb - 1, 0), c)),
            pl.BlockSpec((n_taps, cw), lambda b, i, c: (0, c)),
        ],
        out_specs=pl.BlockSpec((tb, cw), lambda b, i, c: (b * nt + i, c)),
        out_shape=jax.ShapeDtypeStruct((Bp * T, qkv_dim), F32),
        scratch_shapes=[pltpu.VMEM((tb + QKV_HALO, cw), F32)],
        compiler_params=_params(("parallel", "parallel", "parallel"), blk, _nbytes((tb + QKV_HALO, cw), F32)),
        name="qkv_prep_prompt",
    )(proj, proj, w_conv)


def _qkv_prep_sample_body(st_ref, x_ref, w_ref, o_ref, so_ref, *, n_prev, cw, groups_per_kind):
    kind = pl.program_id(0) // groups_per_kind
    x = x_ref[...]
    y = x * w_ref[n_prev:n_prev + 1, :]
    for w in range(n_prev):
        y = y + st_ref[w] * w_ref[w:w + 1, :]
    for w in range(1, n_prev):
        so_ref[w - 1] = st_ref[w]
    so_ref[n_prev - 1] = x

    def finish(static_kind):
        for cb in range(cw // LANES):
            sl = slice(cb * LANES, (cb + 1) * LANES)
            o_ref[:, sl] = _qkv_finish(y[:, sl], static_kind)

    _for_each_kind(kind, finish)


def _qkv_prep_sample(proj, Mp, Bs, qkv_dim, state_t, w_conv):
    n_prev = w_conv.shape[0] - 1
    cw = _pick_tile(qkv_dim // 3, 1024, LANES)
    ncg = qkv_dim // cw
    assert Mp % Bs == 0
    body = functools.partial(_qkv_prep_sample_body, n_prev=n_prev, cw=cw, groups_per_kind=ncg // 3)
    blk = 2 * _nbytes((n_prev, Bs, cw), F32) + 2 * _nbytes((Bs, cw), F32) + _nbytes((SUBLANES, cw), F32)
    return pl.pallas_call(
        body,
        grid=(ncg,),
        in_specs=[pl.BlockSpec((n_prev, Bs, cw), lambda c: (0, 0, c)),
                  pl.BlockSpec((Bs, cw), lambda c: (Mp // Bs, c)),
                  pl.BlockSpec((n_prev + 1, cw), lambda c: (0, c))],
        out_specs=[pl.BlockSpec((Bs, cw), lambda c: (0, c)), pl.BlockSpec((n_prev, Bs, cw), lambda c: (0, 0, c))],
        out_shape=[jax.ShapeDtypeStruct((Bs, qkv_dim), F32), jax.ShapeDtypeStruct((n_prev, Bs, qkv_dim), F32)],
        compiler_params=_params(("parallel",), blk),
        name="qkv_prep_sample",
    )(state_t, proj, w_conv)


def _gates_body(l_ref, alog_ref, dtb_ref, beta_ref, g_ref, *, n_heads, rows, chunk):
    lg = l_ref[...]
    beta_ref[...] = jax.nn.sigmoid(lg[:, :n_heads])
    g = -jnp.exp(alog_ref[...]) * jax.nn.softplus(lg[:, n_heads:] + dtb_ref[...])
    if chunk == 1:
        g_ref[...] = jnp.exp(g)
    else:
        tril = (lax.broadcasted_iota(jnp.int32, (chunk, chunk), 0)
                >= lax.broadcasted_iota(jnp.int32, (chunk, chunk), 1)).astype(F32)
        for c in range(rows // chunk):
            g_ref[c * chunk:(c + 1) * chunk, :] = jnp.dot(
                tril, g[c * chunk:(c + 1) * chunk, :], preferred_element_type=F32, precision=lax.Precision.HIGHEST)


def _gates(logits, a_log, dt_bias, row0, rows, chunk):
    H = logits.shape[1] // 2
    tr = _pick_tile(rows, 512, max(chunk, SUBLANES))
    assert row0 % tr == 0
    body = functools.partial(_gates_body, n_heads=H, rows=tr, chunk=chunk)
    return pl.pallas_call(
        body,
        grid=(rows // tr,),
        in_specs=[pl.BlockSpec((tr, 2 * H), lambda i: (i + row0 // tr, 0)),
                  pl.BlockSpec((1, H), lambda i: (0, 0)), pl.BlockSpec((1, H), lambda i: (0, 0))],
        out_specs=[pl.BlockSpec((tr, H), lambda i: (i, 0)), pl.BlockSpec((tr, H), lambda i: (i, 0))],
        out_shape=[jax.ShapeDtypeStruct((rows, H), F32), jax.ShapeDtypeStruct((rows, H), F32)],
        compiler_params=_params(("parallel",), 4 * _nbytes((tr, LANES), F32)),
        name="gdn_gates",
    )(logits, a_log.reshape(1, H), dt_bias.reshape(1, H))


def _gdn_scan_body(q_ref, k_ref, v_ref, z_ref, beta_ref, g_ref, grow_ref, gon_ref, os_ref, o_ref, s_ref, *,
                   hb, chunks_per_seq, n_chunks):
    step = pl.program_id(1)

    @pl.when(step == n_chunks)
    def _():
        o_ref[0:os_ref.shape[0], :] = os_ref[...]

    @pl.when(step < n_chunks)
    def _():
        @pl.when(step % chunks_per_seq == 0)
        def _():
            s_ref[...] = jnp.zeros_like(s_ref)

        _gdn_chunk(q_ref, k_ref, v_ref, z_ref, beta_ref, g_ref, grow_ref, gon_ref, o_ref, s_ref, hb=hb)


def _gdn_chunk(q_ref, k_ref, v_ref, z_ref, beta_ref, g_ref, grow_ref, gon_ref, o_ref, s_ref, *, hb):
    C = CHUNK

    ri = lax.broadcasted_iota(jnp.int32, (C, C), 0)
    ci = lax.broadcasted_iota(jnp.int32, (C, C), 1)
    causal = ri >= ci
    strict = ri > ci
    gon = gon_ref[...]

    heads = range(hb)
    hs = [slice(h * HEAD_DIM, (h + 1) * HEAD_DIM) for h in heads]
    nt = (((1,), (1,)), ((), ()))
    tn = (((0,), (0,)), ((), ()))
    b16 = lambda xs: [x.astype(BF16) for x in xs]
    mm = lambda xs, ys: [jnp.dot(x, y, preferred_element_type=F32) for x, y in zip(xs, ys)]

    q = [q_ref[:, hs[h]] for h in heads]
    k = [k_ref[:, hs[h]] for h in heads]
    bcol = [beta_ref[0, :, h:h + 1] for h in heads]
    gcol = [g_ref[0, :, h:h + 1] for h in heads]
    grow = [grow_ref[0, h:h + 1, :] for h in heads]
    decay = [jnp.where(causal, jnp.exp(jnp.where(causal, gcol[h] - grow[h], 0.0)), 0.0) for h in heads]
    eg = [jnp.exp(gcol[h]) for h in heads]
    kb = [k[h] * bcol[h] for h in heads]
    k16 = b16(k)
    kk = [lax.dot_general(x, y, nt, preferred_element_type=F32) for x, y in zip(b16(kb), k16)]
    qk = [lax.dot_general(x, y, nt, preferred_element_type=F32) for x, y in zip(b16(q), k16)]
    bm = [jnp.where(strict, kk[h] * decay[h], 0.0) for h in heads]
    attn16 = b16([qk[h] * decay[h] for h in heads])

    half = C // 2
    pr = lax.broadcasted_iota(jnp.int32, (half, C), 0)
    pc = lax.broadcasted_iota(jnp.int32, (half, C), 1) % half
    pblk = lambda s: (pr // s) == (pc // s)
    zero16 = jnp.zeros((half, half), BF16)

    def unpack(ps):
        return [jnp.concatenate([jnp.concatenate([x[:, :half], zero16], axis=1),
                                 jnp.concatenate([zero16, x[:, half:]], axis=1)], axis=0) for x in ps]

    bmp = [jnp.concatenate([x[:half, :half], x[half:, half:]], axis=1) for x in bm]
    in8 = pblk(SUBLANES)
    x0 = [jnp.where(in8, -x, 0.0) for x in bmp]
    tp = [(pr == pc).astype(F32) + x for x in x0]
    x16 = b16(x0)
    p16 = b16(mm(x16, unpack(x16)))
    pf16 = unpack(p16)
    tp = [t + d for t, d in zip(tp, mm(b16(tp), pf16))]
    pf16 = unpack(b16(mm(p16, pf16)))
    tp = [t + d for t, d in zip(tp, mm(b16(tp), pf16))]
    s = SUBLANES
    while s < half:
        m = jnp.logical_and(pblk(2 * s), jnp.logical_not(pblk(s)))
        off16 = unpack(b16([jnp.where(m, x, 0.0) for x in bmp]))
        tp16 = b16(tp)
        tp = [t - d for t, d in zip(tp, mm(b16(mm(tp16, off16)), unpack(tp16)))]
        s *= 2
    ta16 = b16([t[:, :half] for t in tp])
    tb16 = b16([t[:, half:] for t in tp])
    tc16 = b16([-d for d in mm(b16(mm(tb16, b16([x[half:, :half] for x in bm]))), ta16)])
    t16 = [jnp.concatenate([jnp.concatenate([a, zero16], axis=1), jnp.concatenate([c, b], axis=1)], axis=0)
           for a, b, c in zip(ta16, tb16, tc16)]

    v = [v_ref[:, hs[h]] for h in heads]
    sol = mm(t16, [jnp.concatenate([(v[h] * bcol[h]).astype(BF16), (kb[h] * eg[h]).astype(BF16)], axis=1)
                   for h in heads])
    value = [x[:, :HEAD_DIM] for x in sol]
    kcd16 = b16([x[:, HEAD_DIM:] for x in sol])
    st = [s_ref[0, h] for h in heads]
    st16 = b16(st)
    u16 = b16([value[h] - d for h, d in zip(heads, mm(kcd16, st16))])
    o = [a + c for a, c in zip(mm(b16([q[h] * eg[h] for h in heads]), st16), mm(attn16, u16))]
    glast = [gcol[h][C - 1:C, :] for h in heads]
    kd16 = b16([k[h] * jnp.exp(glast[h] - gcol[h]) for h in heads])
    ku = [lax.dot_general(x, y, tn, preferred_element_type=F32) for x, y in zip(kd16, u16)]
    for h in heads:
        s_ref[0, h] = st[h] * jnp.exp(glast[h]) + ku[h]
        on = o[h] * lax.rsqrt(jnp.mean(o[h] * o[h], axis=-1, keepdims=True) + RMS_EPS) * gon
        o_ref[:, hs[h]] = (on * _silu(z_ref[:, hs[h]])).astype(o_ref.dtype)


def _gdn_scan_prompt(qkv, proj, z_col0, beta, gcum, g_onorm, o_sample, Bp, T, H):
    Mp = Bp * T
    Bs = o_sample.shape[0]
    hb = _pick_tile(H, 8, 8) if H % 8 == 0 else H
    tt = CHUNK
    assert T % CHUNK == 0 and (hb % 8 == 0 or hb == H) and Bs <= tt
    nhg, ntt = H // hb, T // tt
    n_chunks = Bp * ntt
    cw = hb * HEAD_DIM
    assert z_col0 % cw == 0
    beta_g = beta.reshape(Mp, nhg, hb).transpose(1, 0, 2)
    gcol_g = gcum.reshape(Mp, nhg, hb).transpose(1, 0, 2)
    grow_g = gcum.reshape(Mp // CHUNK, CHUNK, H).transpose(0, 2, 1)
    body = functools.partial(_gdn_scan_body, hb=hb, chunks_per_seq=ntt, n_chunks=n_chunks)
    row = lambda s: jnp.minimum(s, n_chunks - 1)
    blk = (4 * _nbytes((tt, cw), F32) + 2 * _nbytes((tt, LANES), F32) + _nbytes((hb, LANES), F32)
           + _nbytes((tt, cw), BF16) + _nbytes((Bs, cw), BF16) + _nbytes((hb, HEAD_DIM, HEAD_DIM), F32))
    temporaries = 8 * hb * _nbytes((CHUNK, CHUNK), F32)
    return pl.pallas_call(
        body,
        grid=(nhg, n_chunks + 1),
        in_specs=[
            pl.BlockSpec((tt, cw), lambda h, s: (row(s), h)),
            pl.BlockSpec((tt, cw), lambda h, s: (row(s), nhg + h)),
            pl.BlockSpec((tt, cw), lambda h, s: (row(s), 2 * nhg + h)),
            pl.BlockSpec((tt, cw), lambda h, s: (row(s), z_col0 // cw + h)),
            pl.BlockSpec((1, tt, hb), lambda h, s: (h, row(s), 0)),
            pl.BlockSpec((1, tt, hb), lambda h, s: (h, row(s), 0)),
            pl.BlockSpec((1, hb, CHUNK), lambda h, s: (row(s), h, 0)),
            pl.BlockSpec((1, HEAD_DIM), lambda h, s: (0, 0)),
            pl.BlockSpec((Bs, cw), lambda h, s: (0, h)),
        ],
        out_specs=[
            pl.BlockSpec((tt, cw), lambda h, s: (s, h)),
            pl.BlockSpec((1, hb, HEAD_DIM, HEAD_DIM), lambda h, s: (row(s) // ntt, h, 0, 0)),
        ],
        out_shape=[jax.ShapeDtypeStruct((Mp + Bs, H * HEAD_DIM), BF16),
                   jax.ShapeDtypeStruct((Bp, H, HEAD_DIM, HEAD_DIM), F32)],
        compiler_params=_params(("parallel", "arbitrary"), blk, temporaries),
        name="gdn_scan_prompt",
    )(qkv, qkv, qkv, proj, beta_g, gcol_g, grow_g, g_onorm.reshape(1, HEAD_DIM), o_sample)


def _gdn_step_body(a_ref, beta_ref, s_ref, kt_ref, qt_ref, v_ref, z_ref, gon_ref, so_ref, o_ref, orow_ref, *, n_heads):
    b = pl.program_id(0)
    for h in range(n_heads):
        st = s_ref[0, h]
        a = a_ref[b, h]
        bt = beta_ref[b, h]
        kc = jnp.broadcast_to(kt_ref[0, :, h:h + 1], st.shape)
        qc = jnp.broadcast_to(qt_ref[0, :, h:h + 1], st.shape)
        sk = jnp.sum(st * kc, axis=0, keepdims=True)
        w = v_ref[0, h:h + 1, :] - a * sk
        sn = a * st + (bt * kc) * w
        so_ref[0, h] = sn
        orow_ref[h:h + 1, :] = jnp.sum(sn * qc, axis=0, keepdims=True)
    o = orow_ref[...]
    on = o * lax.rsqrt(jnp.mean(o * o, axis=-1, keepdims=True) + RMS_EPS) * gon_ref[...]
    o_ref[0] = (on * _silu(z_ref[0])).astype(o_ref.dtype)


def _gdn_step_sample(qkv_s, z_s, beta_s, a_s, state, g_onorm):
    Bs, H = beta_s.shape
    hd = H * HEAD_DIM
    q3 = qkv_s[:, :hd].reshape(Bs, H, HEAD_DIM).transpose(0, 2, 1)
    k3 = qkv_s[:, hd:2 * hd].reshape(Bs, H, HEAD_DIM).transpose(0, 2, 1)
    v3 = qkv_s[:, 2 * hd:].reshape(Bs, H, HEAD_DIM)
    z3 = z_s.reshape(Bs, H, HEAD_DIM)
    smem = pl.BlockSpec(memory_space=pltpu.SMEM)
    body = functools.partial(_gdn_step_body, n_heads=H)
    blk = 2 * _nbytes((H, HEAD_DIM, HEAD_DIM), F32) + 6 * _nbytes((HEAD_DIM, LANES), F32)
    so, o = pl.pallas_call(
        body,
        grid=(Bs,),
        in_specs=[smem, smem,
                  pl.BlockSpec((1, H, HEAD_DIM, HEAD_DIM), lambda b: (b, 0, 0, 0)),
                  pl.BlockSpec((1, HEAD_DIM, H), lambda b: (b, 0, 0)),
                  pl.BlockSpec((1, HEAD_DIM, H), lambda b: (b, 0, 0)),
                  pl.BlockSpec((1, H, HEAD_DIM), lambda b: (b, 0, 0)),
                  pl.BlockSpec((1, H, HEAD_DIM), lambda b: (b, 0, 0)),
                  pl.BlockSpec((1, HEAD_DIM), lambda b: (0, 0))],
        out_specs=[pl.BlockSpec((1, H, HEAD_DIM, HEAD_DIM), lambda b: (b, 0, 0, 0)),
                   pl.BlockSpec((1, H, HEAD_DIM), lambda b: (b, 0, 0))],
        out_shape=[jax.ShapeDtypeStruct((Bs, H, HEAD_DIM, HEAD_DIM), F32),
                   jax.ShapeDtypeStruct((Bs, H, HEAD_DIM), BF16)],
        scratch_shapes=[pltpu.VMEM((H, HEAD_DIM), F32)],
        compiler_params=_params(("arbitrary",), blk),
        name="gdn_step_sample",
    )(a_s, beta_s, state, k3, q3, v3, z3, g_onorm.reshape(1, HEAD_DIM))
    return o.reshape(Bs, hd), so


def _seq_tails(a, Bp, T, n, ncols):
    return jnp.stack([lax.slice(a, (b * T + T - n, 0), ((b + 1) * T, ncols)) for b in range(Bp)])


def _wcol_bytes(*ws):
    return sum(w.shape[1] * w.dtype.itemsize for w in ws)


def _conformer_layer(h, xn, g_next, Mp, Bp, T, Bs, state, j, w_pw1, w_dw, b_dw, ln_g, ln_b, w_pw2):
    M, D = h.shape
    Dc = w_pw1.shape[2] // 2
    tm, tn = _mm_tiles(M, D, Dc, 2 * _wcol_bytes(w_pw1))
    glu = _matmul(_mm_glu_body, "mm_glu", M, Dc, tm, tn,
                  _xn_specs(xn, tm) + [_w_spec(w_pw1, j, tn), _w_spec(w_pw1, j, tn, Dc // tn)], F32)
    c_s, new_s = _conf_conv_sample(glu, Mp, Bs, jnp.swapaxes(state, 0, 1), w_dw, b_dw, ln_g, ln_b)
    new_s = jnp.swapaxes(new_s, 0, 1)
    c = _conf_conv_prompt(glu, Bp, T, c_s, w_dw, b_dw, ln_g, ln_b)
    tm, tn = _mm_tiles(M, Dc, D, _wcol_bytes(w_pw2))
    h, xn = _matmul(_mm_res_body, "mm_pw2", M, D, tm, tn,
                    [_x_spec(c, tm), _w_spec(w_pw2, j, tn), _r_spec(h, tm, tn), _g_spec(g_next, tn)], F32, True)
    new_p = _seq_tails(glu, Bp, T, w_dw.shape[0] - 1, Dc)
    return h, xn, new_p, new_s


def _gdn_layer(h, xn, g_next, Mp, Bp, T, Bs, conv_state, gdn_state, j, w_in, w_conv, a_log, dt_bias, g_onorm, w_out):
    M, D = h.shape
    H = a_log.shape[0]
    hd = H * HEAD_DIM
    qkv_dim = 3 * hd
    n_main = qkv_dim + hd
    tm, tn = _mm_tiles(M, D, n_main, _wcol_bytes(w_in))
    w_in_t = jnp.swapaxes(w_in, 1, 2)
    wt_spec = lambda rows, row_block0: (w_in_t, (None, rows, D), lambda i, c: (j, c + row_block0, 0))
    proj = _matmul(_mm_plain_wt_body, "mm_gdn_in", M, n_main, tm, tn, _xn_specs(xn, tm) + [wt_spec(tn, 0)], F32)
    assert n_main % (2 * H) == 0
    logits = _matmul(_mm_plain_wt_body, "mm_gdn_gates", M, 2 * H, tm, 2 * H,
                     _xn_specs(xn, tm) + [wt_spec(2 * H, n_main // (2 * H))], F32)

    beta_p, gcum_p = _gates(logits, a_log, dt_bias, 0, Mp, CHUNK)
    beta_s, a_s = _gates(logits, a_log, dt_bias, Mp, Bs, 1)

    qkv_s, new_conv_s = _qkv_prep_sample(proj, Mp, Bs, qkv_dim, jnp.swapaxes(conv_state, 0, 1), w_conv)
    new_conv_s = jnp.swapaxes(new_conv_s, 0, 1)
    o_s, s_s = _gdn_step_sample(qkv_s, lax.slice(proj, (Mp, qkv_dim), (M, n_main)), beta_s, a_s, gdn_state, g_onorm)

    qkv_p = _qkv_prep_prompt(proj, Bp, T, qkv_dim, w_conv)
    o, s_p = _gdn_scan_prompt(qkv_p, proj, qkv_dim, beta_p, gcum_p, g_onorm, o_s, Bp, T, H)
    tm, tn = _mm_tiles(M, hd, D, _wcol_bytes(w_out))
    h, xn = _matmul(_mm_res_body, "mm_gdn_out", M, D, tm, tn,
                    [_x_spec(o, tm), _w_spec(w_out, j, tn), _r_spec(h, tm, tn), _g_spec(g_next, tn)], F32, True)
    new_conv_p = _seq_tails(proj, Bp, T, w_conv.shape[0] - 1, qkv_dim)
    return h, xn, new_conv_p, new_conv_s, s_p, s_s


def _ffn_ple(h, xn, p_all, i, w_gate, w_up, w_down, g_ple, w_pg, w_pp, g_next):
    M, D = h.shape
    F = w_gate.shape[2]
    tm, tn = _mm_tiles(M, D, F, _wcol_bytes(w_gate, w_up))
    up_operands = _xn_specs(xn, tm) + [_w_spec(w_gate, i, tn), _w_spec(w_up, i, tn)]
    n_steps = (M // tm) * (F // tn)
    slab = F // n_steps
    if F % n_steps == 0 and slab % 16 == 0:
        nj = F // tn
        act, w_down16 = _matmul(
            _mm_swiglu_cast_body, "mm_ffn_up", M, F, tm, tn,
            up_operands + [(w_down, (None, slab, D), lambda r, c: (i, r * nj + c, 0))], BF16,
            side_out=((F, D), BF16, (slab, D), lambda r, c: (r * nj + c, 0)))
    else:
        act = _matmul(_mm_swiglu_body, "mm_ffn_up", M, F, tm, tn, up_operands, BF16)
        w_down16 = w_down[i].astype(BF16)
    w_down16 = w_down16[None]
    tm, tn = _mm_tiles(M, F, D, _wcol_bytes(w_down16))
    h, xn = _matmul(_mm_res_body, "mm_ffn_down", M, D, tm, tn,
                    [_x_spec(act, tm), _w_spec(w_down16, 0, tn), _r_spec(h, tm, tn), _g_spec(g_ple, tn)], F32, True)
    tm, tn = _mm_tiles(M, D, D, _wcol_bytes(w_pg, w_pp))
    p_spec = (p_all, (None, tm, p_all.shape[2]), lambda r, c: (i, r, 0))
    operands = _xn_specs(xn, tm) + [p_spec, _w_spec(w_pg, i, tn), _w_spec(w_pp, i, tn), _r_spec(h, tm, tn)]
    if g_next is None:
        return _matmul(_mm_ple_last_body, "mm_ple", M, D, tm, tn, operands, F32), None
    return _matmul(_mm_ple_body, "mm_ple", M, D, tm, tn, operands + [_g_spec(g_next, tn)], F32, True)


def kernel(x_prompt, x_sample, p_prompt, p_sample, state_conv_conformer, state_conv_qkv, state_gdn, g_mix, g_ffn, g_ple, g_final, conf_w_pw1, conf_w_dw, conf_b_dw, conf_ln_g, conf_ln_b, conf_w_pw2, gdn_w_in, gdn_w_conv, gdn_a_log, gdn_dt_bias, gdn_g_onorm, gdn_w_out, ffn_w_gate, ffn_w_up, ffn_w_down, ple_w_gate, ple_w_proj):
    Bp, T, D = x_prompt.shape
    Bs, Ts, _ = x_sample.shape
    assert Ts == 1
    depth = g_mix.shape[0]
    Mp = Bp * T
    M = Mp + Bs
    h, xn = _embed(x_prompt.reshape(Mp, D), x_sample.reshape(Bs, D), g_mix[0])
    p_all = jnp.concatenate([p_prompt.reshape(depth, Mp, -1), p_sample.reshape(depth, Bs, -1)], axis=1).astype(BF16)
    conf_p, conf_s, qkv_p, qkv_s, gdn_p, gdn_s = [], [], [], [], [], []
    for i in range(depth):
        j = i // 2
        if i % 2 == 0:
            h, xn, np_, ns_ = _conformer_layer(h, xn, g_ffn[i], Mp, Bp, T, Bs, state_conv_conformer[j], j, conf_w_pw1,
                                               conf_w_dw[j], conf_b_dw[j], conf_ln_g[j], conf_ln_b[j], conf_w_pw2)
            conf_p.append(np_)
            conf_s.append(ns_)
        else:
            h, xn, cp_, cs_, sp_, ss_ = _gdn_layer(h, xn, g_ffn[i], Mp, Bp, T, Bs, state_conv_qkv[j], state_gdn[j], j,
                                                   gdn_w_in, gdn_w_conv[j], gdn_a_log[j], gdn_dt_bias[j],
                                                   gdn_g_onorm[j], gdn_w_out)
            qkv_p.append(cp_)
            qkv_s.append(cs_)
            gdn_p.append(sp_)
            gdn_s.append(ss_)
        h, xn = _ffn_ple(h, xn, p_all, i, ffn_w_gate, ffn_w_up, ffn_w_down, g_ple[i], ple_w_gate, ple_w_proj,
                         g_mix[i + 1] if i + 1 < depth else None)

    y_p = _rmsnorm(h, g_final, F32, 0, Mp).reshape(Bp, T, D)
    y_s = _rmsnorm(h, g_final, F32, Mp, Bs).reshape(Bs, 1, D)
    return (y_p, y_s, jnp.stack(conf_p), jnp.stack(qkv_p), jnp.stack(gdn_p),
            jnp.stack(conf_s), jnp.stack(qkv_s), jnp.stack(gdn_s))
```

```python
import functools

import jax
import jax.numpy as jnp
from jax import lax
from jax.experimental import pallas as pl
from jax.experimental.pallas import tpu as pltpu

F32 = jnp.float32
BF16 = jnp.bfloat16

RMS_EPS = 1e-6
LN_EPS = 1e-5
L2_EPS = 1e-6
CHUNK = 256
HEAD_DIM = 128
LANES = 128
SUBLANES = 8
V7X_VMEM_LIMIT_CAP = 60000 * 1024
VMEM_INTERNAL_SCRATCH = 8 * 2**20


def _pick_tile(n, cap, mult):
    best = None
    for d in range(mult, min(n, cap) + 1, mult):
        if n % d == 0:
            best = d
    return n if best is None else best


def _params(sem, block_bytes, scratch_bytes=0):
    need = 2 * block_bytes + scratch_bytes + VMEM_INTERNAL_SCRATCH
    return pltpu.CompilerParams(dimension_semantics=sem,
                                vmem_limit_bytes=int(min(max(need, 32 * 2**20), V7X_VMEM_LIMIT_CAP)))


def _nbytes(shape, dtype):
    n = 1
    for s in shape:
        n *= 1 if s is None else s
    return n * jnp.dtype(dtype).itemsize


def _bdot(a, b):
    return jnp.dot(a.astype(BF16), b.astype(BF16), preferred_element_type=F32)


def _silu(x):
    return x * jax.nn.sigmoid(x)


def _rms_body(x_ref, g_ref, o_ref):
    x = x_ref[...]
    y = x * lax.rsqrt(jnp.mean(x * x, axis=-1, keepdims=True) + RMS_EPS)
    o_ref[...] = (y * g_ref[...]).astype(o_ref.dtype)


def _rmsnorm(x, g, out_dtype, row0=0, rows=None):
    M, D = x.shape
    rows = M if rows is None else rows
    tm = _pick_tile(rows, 320, 16)
    assert row0 % tm == 0
    blk = _nbytes((tm, D), F32) + _nbytes((tm, D), out_dtype)
    return pl.pallas_call(
        _rms_body,
        grid=(rows // tm,),
        in_specs=[pl.BlockSpec((tm, D), lambda i: (i + row0 // tm, 0)),
                  pl.BlockSpec((1, D), lambda i: (0, 0))],
        out_specs=pl.BlockSpec((tm, D), lambda i: (i, 0)),
        out_shape=jax.ShapeDtypeStruct((rows, D), out_dtype),
        compiler_params=_params(("parallel",), blk),
        name="rmsnorm",
    )(x, g.reshape(1, D))


def _embed_body(xp_ref, xs_ref, g_ref, h_ref, xg_ref, ss_ref, *, n_prompt_tiles):
    def emit(x):
        h_ref[...] = x
        xg_ref[...] = (x * g_ref[...]).astype(xg_ref.dtype)
        ss_ref[...] = jnp.sum(x * x, axis=-1, keepdims=True)

    @pl.when(pl.program_id(0) < n_prompt_tiles)
    def _():
        emit(xp_ref[...])

    @pl.when(pl.program_id(0) >= n_prompt_tiles)
    def _():
        emit(xs_ref[...])


def _embed(xp, xs, g):
    (Mp, D), Bs = xp.shape, xs.shape[0]
    tr = Bs
    assert Mp % tr == 0 and tr % SUBLANES == 0
    npt = Mp // tr
    body = functools.partial(_embed_body, n_prompt_tiles=npt)
    blk = 3 * _nbytes((tr, D), F32) + _nbytes((tr, D), BF16) + _nbytes((tr, LANES), F32)
    h, xg, ss = pl.pallas_call(
        body,
        grid=(npt + 1,),
        in_specs=[pl.BlockSpec((tr, D), lambda i: (jnp.minimum(i, npt - 1), 0)),
                  pl.BlockSpec((tr, D), lambda i: (0, 0)),
                  pl.BlockSpec((1, D), lambda i: (0, 0))],
        out_specs=[pl.BlockSpec((tr, D), lambda i: (i, 0)), pl.BlockSpec((tr, D), lambda i: (i, 0)),
                   pl.BlockSpec((tr, 1), lambda i: (i, 0))],
        out_shape=[jax.ShapeDtypeStruct((Mp + Bs, D), F32), jax.ShapeDtypeStruct((Mp + Bs, D), BF16),
                   jax.ShapeDtypeStruct((Mp + Bs, 1), F32)],
        compiler_params=_params(("parallel",), blk),
        name="embed_norm",
    )(xp, xs, g.reshape(1, D))
    return h, (xg, ss)


def _wdot(x_ref, w_ref):
    return jnp.dot(x_ref[...], w_ref[...].astype(BF16), preferred_element_type=F32)


def _row_scale(ss_ref, width):
    return lax.rsqrt(ss_ref[...] * (1.0 / width) + RMS_EPS)


def _emit_normalised(val, g_ref, o_ref, xg_ref, ss_ref):
    o_ref[...] = val
    xg_ref[...] = (val * g_ref[...]).astype(xg_ref.dtype)
    part = jnp.sum(val * val, axis=-1, keepdims=True)
    j = pl.program_id(1)

    @pl.when(j == 0)
    def _():
        ss_ref[...] = part

    @pl.when(j > 0)
    def _():
        ss_ref[...] += part


def _mm_glu_body(x_ref, ss_ref, wa_ref, wg_ref, o_ref):
    r = _row_scale(ss_ref, x_ref.shape[1])
    o_ref[...] = ((_wdot(x_ref, wa_ref) * r) * jax.nn.sigmoid(_wdot(x_ref, wg_ref) * r)).astype(o_ref.dtype)


def _mm_swiglu_body(x_ref, ss_ref, wg_ref, wu_ref, o_ref):
    r = _row_scale(ss_ref, x_ref.shape[1])
    o_ref[...] = (_silu(_wdot(x_ref, wg_ref) * r) * (_wdot(x_ref, wu_ref) * r)).astype(o_ref.dtype)


def _mm_swiglu_cast_body(x_ref, ss_ref, wg_ref, wu_ref, wd_ref, o_ref, wd16_ref):
    _mm_swiglu_body(x_ref, ss_ref, wg_ref, wu_ref, o_ref)
    wd16_ref[...] = wd_ref[...].astype(wd16_ref.dtype)


def _mm_plain_body(x_ref, ss_ref, w_ref, o_ref):
    o_ref[...] = (_wdot(x_ref, w_ref) * _row_scale(ss_ref, x_ref.shape[1])).astype(o_ref.dtype)


def _mm_plain_wt_body(x_ref, ss_ref, wt_ref, o_ref):
    d = lax.dot_general(x_ref[...], wt_ref[...].astype(BF16), (((1,), (1,)), ((), ())), preferred_element_type=F32)
    o_ref[...] = (d * _row_scale(ss_ref, x_ref.shape[1])).astype(o_ref.dtype)


def _mm_res_body(x_ref, w_ref, r_ref, g_ref, o_ref, xg_ref, sso_ref):
    _emit_normalised(r_ref[...] + _wdot(x_ref, w_ref), g_ref, o_ref, xg_ref, sso_ref)


def _ple_value(x_ref, ss_ref, p_ref, wg_ref, wp_ref, r_ref):
    gate = jax.nn.sigmoid(_wdot(x_ref, wg_ref) * _row_scale(ss_ref, x_ref.shape[1]))
    return r_ref[...] + gate * _wdot(p_ref, wp_ref)


def _mm_ple_body(x_ref, ss_ref, p_ref, wg_ref, wp_ref, r_ref, g_ref, o_ref, xg_ref, sso_ref):
    _emit_normalised(_ple_value(x_ref, ss_ref, p_ref, wg_ref, wp_ref, r_ref), g_ref, o_ref, xg_ref, sso_ref)


def _mm_ple_last_body(x_ref, ss_ref, p_ref, wg_ref, wp_ref, r_ref, o_ref):
    o_ref[...] = _ple_value(x_ref, ss_ref, p_ref, wg_ref, wp_ref, r_ref)


def _matmul(body, name, M, N, tm, tn, operands, out_dtype, emit_normalised=False, side_out=None):
    blk = sum(_nbytes(bs, a.dtype) for a, bs, _ in operands) + _nbytes((tm, tn), out_dtype)
    out_specs = [pl.BlockSpec((tm, tn), lambda i, j: (i, j))]
    out_shape = [jax.ShapeDtypeStruct((M, N), out_dtype)]
    if emit_normalised:
        out_specs += [pl.BlockSpec((tm, tn), lambda i, j: (i, j)), pl.BlockSpec((tm, 1), lambda i, j: (i, 0))]
        out_shape += [jax.ShapeDtypeStruct((M, N), BF16), jax.ShapeDtypeStruct((M, 1), F32)]
        blk += _nbytes((tm, tn), BF16) + _nbytes((tm, LANES), F32)
    if side_out is not None:
        out_specs.append(pl.BlockSpec(side_out[2], side_out[3]))
        out_shape.append(jax.ShapeDtypeStruct(side_out[0], side_out[1]))
        blk += _nbytes(side_out[2], side_out[1])
    outs = pl.pallas_call(
        body,
        grid=(M // tm, N // tn),
        in_specs=[pl.BlockSpec(bs, im) for _, bs, im in operands],
        out_specs=out_specs,
        out_shape=out_shape,
        compiler_params=_params(("parallel", "arbitrary"), blk),
        name=name,
    )(*[a for a, _, _ in operands])
    main = (outs[0], (outs[1], outs[2])) if emit_normalised else outs[0]
    return main if side_out is None else (main, outs[-1])


def _x_spec(x, tm):
    return (x, (tm, x.shape[1]), lambda i, j: (i, 0))


def _xn_specs(xn, tm):
    xg, ss = xn
    return [_x_spec(xg, tm), (ss, (tm, 1), lambda i, j: (i, 0))]


def _w_spec(w, layer, tn, col_block0=0):
    return (w, (None, w.shape[1], tn), lambda i, j: (layer, 0, j + col_block0))


def _r_spec(r, tm, tn):
    return (r, (tm, tn), lambda i, j: (i, j))


def _g_spec(g, tn):
    return (g.reshape(1, -1), (1, tn), lambda i, j: (0, j))


def _mm_tiles(M, K, N, w_bytes_per_col):
    tn = _pick_tile(N, 512 if 512 * w_bytes_per_col <= 9 * 2**20 else 256, LANES)
    tm_cap = 1040 if K <= 4096 else 640
    return _pick_tile(M, tm_cap, 16), tn


def _conv_blocks(load_blk, w_row, n_taps, n_out, width):
    qmax = (n_taps - 1) // SUBLANES
    rows = lax.broadcasted_iota(jnp.int32, (SUBLANES, width), 0)
    x = {k: load_blk(k) for k in range(-(qmax + 1), n_out)}
    acc = [None] * n_out
    for r in range(min(SUBLANES, n_taps)):
        if r == 0:
            y = x
        else:
            rot = {k: pltpu.roll(v, r, 0) for k, v in x.items()}
            y = {k: jnp.where(rows >= r, rot[k], rot[k - 1]) for k in range(-qmax, n_out)}
        for q in range(qmax + 1):
            d = SUBLANES * q + r
            if d >= n_taps:
                continue
            w = w_row(d)
            for j in range(n_out):
                t = y[j - q] * w
                acc[j] = t if acc[j] is None else acc[j] + t
    return acc


CONF_HALO = 32
CONV_STRIP = 8
CONV_ROWS_CAP = 256


def _conf_conv_body(x_ref, halo_ref, cs_ref, w_ref, b_ref, lg_ref, lb_ref, o_ref, xe_ref, y_ref, *,
                    tb, ncb, n_taps, tiles_per_seq, n_tiles):
    step = pl.program_id(0)

    @pl.when(step == n_tiles)
    def _():
        o_ref[0:cs_ref.shape[0], :] = cs_ref[...].astype(o_ref.dtype)

    @pl.when(step < n_tiles)
    def _():
        _conf_conv_tile(x_ref, halo_ref, w_ref, b_ref, lg_ref, lb_ref, o_ref, xe_ref, y_ref,
                        first=step % tiles_per_seq == 0, tb=tb, ncb=ncb, n_taps=n_taps)


def _conf_conv_tile(x_ref, halo_ref, w_ref, b_ref, lg_ref, lb_ref, o_ref, xe_ref, y_ref, *, first, tb, ncb, n_taps):
    for cb in range(ncb):
        sl = slice(cb * LANES, (cb + 1) * LANES)
        xe_ref[cb, CONF_HALO:CONF_HALO + tb, :] = x_ref[:, sl]
        xe_ref[cb, 0:CONF_HALO, :] = jnp.where(first, 0.0, halo_ref[:, sl])

    strip_rows = SUBLANES * CONV_STRIP
    n_strips = tb // strip_rows

    def col_loop(cb, carry):
        def strip_loop(s, carry2):
            r0 = pl.multiple_of(s * strip_rows, strip_rows)
            acc = _conv_blocks(
                lambda k: xe_ref[cb, pl.ds(r0 + (CONF_HALO + SUBLANES * k), SUBLANES), :],
                lambda d: w_ref[cb, pl.ds(n_taps - 1 - d, 1), :],
                n_taps, CONV_STRIP, LANES)
            bias = b_ref[cb]
            for j in range(CONV_STRIP):
                y_ref[cb, pl.ds(r0 + SUBLANES * j, SUBLANES), :] = acc[j] + bias
            return carry2
        return lax.fori_loop(0, n_strips, strip_loop, carry)

    lax.fori_loop(0, ncb, col_loop, 0)

    ln_rows = 32
    inv_c = 1.0 / (ncb * LANES)

    def ln_loop(s, carry):
        r0 = pl.multiple_of(s * ln_rows, ln_rows)
        y = y_ref[:, pl.ds(r0, ln_rows), :]
        mu = jnp.sum(jnp.sum(y, axis=0), axis=-1, keepdims=True) * inv_c
        yc = y - mu[None]
        var = jnp.sum(jnp.sum(yc * yc, axis=0), axis=-1, keepdims=True) * inv_c
        rstd = lax.rsqrt(var + LN_EPS)
        for cb in range(ncb):
            t = yc[cb] * rstd * lg_ref[cb] + lb_ref[cb]
            o_ref[pl.ds(r0, ln_rows), cb * LANES:(cb + 1) * LANES] = _silu(t).astype(o_ref.dtype)
        return carry

    lax.fori_loop(0, tb // ln_rows, ln_loop, 0)


def _col_major(v, pad_rows=None):
    R, C = v.shape
    out = v.reshape(R, C // LANES, LANES).transpose(1, 0, 2)
    if pad_rows is not None and pad_rows > R:
        out = jnp.pad(out, ((0, 0), (0, pad_rows - R), (0, 0)))
    return out


def _conf_conv_prompt(glu, Bp, T, c_sample, w_dw, b_dw, ln_g, ln_b):
    Dc = glu.shape[1]
    Bs = c_sample.shape[0]
    n_taps = w_dw.shape[0]
    assert n_taps - 1 <= CONF_HALO and T >= CONF_HALO
    tb = _pick_tile(T, CONV_ROWS_CAP, SUBLANES * CONV_STRIP)
    assert Bs <= tb
    ncb = Dc // LANES
    n_tiles = Bp * T // tb
    body = functools.partial(_conf_conv_body, tb=tb, ncb=ncb, n_taps=n_taps, tiles_per_seq=T // tb, n_tiles=n_tiles)
    hb = tb // CONF_HALO
    blk = (_nbytes((tb, Dc), F32) + _nbytes((CONF_HALO, Dc), F32) + _nbytes((tb, Dc), BF16) + _nbytes((Bs, Dc), F32)
           + _nbytes((ncb, 32 + 3, LANES), F32))
    scratch = _nbytes((ncb, tb + CONF_HALO, LANES), F32) + _nbytes((ncb, tb, LANES), F32)
    tile = lambda s: jnp.minimum(s, n_tiles - 1)
    const3 = lambda s: (0, 0, 0)
    return pl.pallas_call(
        body,
        grid=(n_tiles + 1,),
        in_specs=[
            pl.BlockSpec((tb, Dc), lambda s: (tile(s), 0)),
            pl.BlockSpec((CONF_HALO, Dc), lambda s: (jnp.maximum(tile(s) * hb - 1, 0), 0)),
            pl.BlockSpec((Bs, Dc), lambda s: (0, 0)),
            pl.BlockSpec((ncb, 32, LANES), const3),
            pl.BlockSpec((ncb, 1, LANES), const3),
            pl.BlockSpec((ncb, 1, LANES), const3),
            pl.BlockSpec((ncb, 1, LANES), const3),
        ],
        out_specs=pl.BlockSpec((tb, Dc), lambda s: (s, 0)),
        out_shape=jax.ShapeDtypeStruct((Bp * T + Bs, Dc), BF16),
        scratch_shapes=[pltpu.VMEM((ncb, tb + CONF_HALO, LANES), F32), pltpu.VMEM((ncb, tb, LANES), F32)],
        compiler_params=_params(("arbitrary",), blk, scratch),
        name="conf_conv_prompt",
    )(glu, glu, c_sample, _col_major(w_dw, 32), _col_major(b_dw[None]), _col_major(ln_g[None]), _col_major(ln_b[None]))


def _conf_conv_sample_body(st_ref, x_ref, w_ref, b_ref, lg_ref, lb_ref, o_ref, so_ref, acc_ref, *, n_prev):
    w = pl.program_id(0)

    @pl.when(w == 0)
    def _():
        acc_ref[...] = x_ref[...] * w_ref[pl.ds(n_prev, 1), :]

    @pl.when(w < n_prev)
    def _():
        st = st_ref[...]
        acc_ref[...] += st * w_ref[pl.ds(w, 1), :]
        so_ref[...] = st

    @pl.when(w == n_prev)
    def _():
        so_ref[...] = x_ref[...]
        y = acc_ref[...] + b_ref[...]
        mu = jnp.mean(y, axis=-1, keepdims=True)
        yc = y - mu
        rstd = lax.rsqrt(jnp.mean(yc * yc, axis=-1, keepdims=True) + LN_EPS)
        o_ref[...] = _silu(yc * rstd * lg_ref[...] + lb_ref[...])


def _conf_conv_sample(glu, Mp, Bs, state_t, w_dw, b_dw, ln_g, ln_b):
    Dc = glu.shape[1]
    n_taps = w_dw.shape[0]
    n_prev = n_taps - 1
    assert Mp % Bs == 0
    blk = 4 * _nbytes((Bs, Dc), F32) + _nbytes((n_taps + 3 * SUBLANES, Dc), F32)
    row = lambda w: (0, 0)
    body = functools.partial(_conf_conv_sample_body, n_prev=n_prev)
    return pl.pallas_call(
        body,
        grid=(n_prev + 1,),
        in_specs=[
            pl.BlockSpec((None, Bs, Dc), lambda w: (jnp.minimum(w, n_prev - 1), 0, 0)),
            pl.BlockSpec((Bs, Dc), lambda w: (Mp // Bs, 0)),
            pl.BlockSpec((n_taps, Dc), row),
            pl.BlockSpec((1, Dc), row), pl.BlockSpec((1, Dc), row), pl.BlockSpec((1, Dc), row),
        ],
        out_specs=[pl.BlockSpec((Bs, Dc), lambda w: (0, 0)),
                   pl.BlockSpec((None, Bs, Dc), lambda w: (jnp.maximum(w - 1, 0), 0, 0))],
        out_shape=[jax.ShapeDtypeStruct((Bs, Dc), F32), jax.ShapeDtypeStruct((n_prev, Bs, Dc), F32)],
        scratch_shapes=[pltpu.VMEM((Bs, Dc), F32)],
        compiler_params=_params(("arbitrary",), blk, _nbytes((Bs, Dc), F32)),
        name="conf_conv_sample",
    )(state_t, glu, w_dw, b_dw[None], ln_g[None], ln_b[None])


def _qkv_finish(y, kind):
    y = _silu(y)
    if kind == 2:
        return y
    fac = lax.rsqrt(jnp.sum(y * y, axis=-1, keepdims=True) + L2_EPS)
    return y * (fac * HEAD_DIM ** -0.5 if kind == 0 else fac)


def _for_each_kind(kind, fn):
    for k in range(3):
        pl.when(kind == k)(functools.partial(fn, k))


def _qkv_prep_sample_body(st_ref, x_ref, w_ref, o_ref, so_ref, *, n_prev, cw, groups_per_kind):
    kind = pl.program_id(0) // groups_per_kind
    x = x_ref[...]
    y = x * w_ref[n_prev:n_prev + 1, :]
    for w in range(n_prev):
        y = y + st_ref[w] * w_ref[w:w + 1, :]
    for w in range(1, n_prev):
        so_ref[w - 1] = st_ref[w]
    so_ref[n_prev - 1] = x

    def finish(static_kind):
        for cb in range(cw // LANES):
            sl = slice(cb * LANES, (cb + 1) * LANES)
            o_ref[:, sl] = _qkv_finish(y[:, sl], static_kind)

    _for_each_kind(kind, finish)


def _qkv_prep_sample(proj, Mp, Bs, qkv_dim, state_t, w_conv):
    n_prev = w_conv.shape[0] - 1
    cw = _pick_tile(qkv_dim // 3, 1024, LANES)
    ncg = qkv_dim // cw
    assert Mp % Bs == 0
    body = functools.partial(_qkv_prep_sample_body, n_prev=n_prev, cw=cw, groups_per_kind=ncg // 3)
    blk = 2 * _nbytes((n_prev, Bs, cw), F32) + 2 * _nbytes((Bs, cw), F32) + _nbytes((SUBLANES, cw), F32)
    return pl.pallas_call(
        body,
        grid=(ncg,),
        in_specs=[pl.BlockSpec((n_prev, Bs, cw), lambda c: (0, 0, c)),
                  pl.BlockSpec((Bs, cw), lambda c: (Mp // Bs, c)),
                  pl.BlockSpec((n_prev + 1, cw), lambda c: (0, c))],
        out_specs=[pl.BlockSpec((Bs, cw), lambda c: (0, c)), pl.BlockSpec((n_prev, Bs, cw), lambda c: (0, 0, c))],
        out_shape=[jax.ShapeDtypeStruct((Bs, qkv_dim), F32), jax.ShapeDtypeStruct((n_prev, Bs, qkv_dim), F32)],
        compiler_params=_params(("parallel",), blk),
        name="qkv_prep_sample",
    )(state_t, proj, w_conv)


def _gates_body(l_ref, alog_ref, dtb_ref, beta_ref, g_ref, *, n_heads, rows, chunk):
    lg = l_ref[...]
    beta_ref[...] = jax.nn.sigmoid(lg[:, :n_heads])
    g = -jnp.exp(alog_ref[...]) * jax.nn.softplus(lg[:, n_heads:] + dtb_ref[...])
    if chunk == 1:
        g_ref[...] = jnp.exp(g)
    else:
        tril = (lax.broadcasted_iota(jnp.int32, (chunk, chunk), 0)
                >= lax.broadcasted_iota(jnp.int32, (chunk, chunk), 1)).astype(F32)
        for c in range(rows // chunk):
            g_ref[c * chunk:(c + 1) * chunk, :] = jnp.dot(
                tril, g[c * chunk:(c + 1) * chunk, :], preferred_element_type=F32, precision=lax.Precision.HIGHEST)


def _gates(logits, a_log, dt_bias, row0, rows, chunk):
    H = logits.shape[1] // 2
    tr = _pick_tile(rows, 512, max(chunk, SUBLANES))
    assert row0 % tr == 0
    body = functools.partial(_gates_body, n_heads=H, rows=tr, chunk=chunk)
    return pl.pallas_call(
        body,
        grid=(rows // tr,),
        in_specs=[pl.BlockSpec((tr, 2 * H), lambda i: (i + row0 // tr, 0)),
                  pl.BlockSpec((1, H), lambda i: (0, 0)), pl.BlockSpec((1, H), lambda i: (0, 0))],
        out_specs=[pl.BlockSpec((tr, H), lambda i: (i, 0)), pl.BlockSpec((tr, H), lambda i: (i, 0))],
        out_shape=[jax.ShapeDtypeStruct((rows, H), F32), jax.ShapeDtypeStruct((rows, H), F32)],
        compiler_params=_params(("parallel",), 4 * _nbytes((tr, LANES), F32)),
        name="gdn_gates",
    )(logits, a_log.reshape(1, H), dt_bias.reshape(1, H))


def _gdn_scan_body(q_ref, k_ref, v_ref, qh_ref, kh_ref, vh_ref, wq_ref, wk_ref, wv_ref, z_ref, beta_ref, g_ref,
                   grow_ref, gon_ref, os_ref, o_ref, s_ref, *, hb, chunks_per_seq, n_chunks, n_taps):
    step = pl.program_id(1)

    @pl.when(step == n_chunks)
    def _():
        o_ref[0:os_ref.shape[0], :] = os_ref[...]

    @pl.when(step < n_chunks)
    def _():
        @pl.when(step % chunks_per_seq == 0)
        def _():
            s_ref[...] = jnp.zeros_like(s_ref)

        raw = ((q_ref, qh_ref, wq_ref), (k_ref, kh_ref, wk_ref), (v_ref, vh_ref, wv_ref))
        _gdn_chunk(raw, z_ref, beta_ref, g_ref, grow_ref, gon_ref, o_ref, s_ref, hb=hb, n_taps=n_taps,
                   first=step % chunks_per_seq == 0)


def _gdn_chunk(raw, z_ref, beta_ref, g_ref, grow_ref, gon_ref, o_ref, s_ref, *, hb, n_taps, first):
    C = CHUNK

    def prepared(kind, h):
        x_ref, halo_ref, w_ref = raw[kind]
        sl = slice(h * HEAD_DIM, (h + 1) * HEAD_DIM)
        halo = jnp.where(first, 0.0, halo_ref[:, sl])
        acc = _conv_blocks(lambda b: halo if b < 0 else x_ref[SUBLANES * b:SUBLANES * (b + 1), sl],
                           lambda d: w_ref[n_taps - 1 - d:n_taps - d, sl], n_taps, C // SUBLANES, HEAD_DIM)
        return jnp.concatenate([_qkv_finish(a, kind) for a in acc], axis=0)

    ri = lax.broadcasted_iota(jnp.int32, (C, C), 0)
    ci = lax.broadcasted_iota(jnp.int32, (C, C), 1)
    causal = ri >= ci
    strict = ri > ci
    gon = gon_ref[...]

    heads = range(hb)
    hs = [slice(h * HEAD_DIM, (h + 1) * HEAD_DIM) for h in heads]
    nt = (((1,), (1,)), ((), ()))
    tn = (((0,), (0,)), ((), ()))
    b16 = lambda xs: [x.astype(BF16) for x in xs]
    mm = lambda xs, ys: [jnp.dot(x, y, preferred_element_type=F32) for x, y in zip(xs, ys)]

    k = [prepared(1, h) for h in heads]
    bcol = [beta_ref[0, :, h:h + 1] for h in heads]
    gcol = [g_ref[0, :, h:h + 1] for h in heads]
    grow = [grow_ref[0, h:h + 1, :] for h in heads]
    decay = [jnp.where(causal, jnp.exp(jnp.where(causal, gcol[h] - grow[h], 0.0)), 0.0) for h in heads]
    eg = [jnp.exp(gcol[h]) for h in heads]
    kb = [k[h] * bcol[h] for h in heads]
    k16 = b16(k)
    kk = [lax.dot_general(x, y, nt, preferred_element_type=F32) for x, y in zip(b16(kb), k16)]
    bm = [jnp.where(strict, kk[h] * decay[h], 0.0) for h in heads]

    half = C // 2
    pr = lax.broadcasted_iota(jnp.int32, (half, C), 0)
    pc = lax.broadcasted_iota(jnp.int32, (half, C), 1) % half
    pblk = lambda s: (pr // s) == (pc // s)
    zero16 = jnp.zeros((half, half), BF16)

    def unpack(ps):
        return [jnp.concatenate([jnp.concatenate([x[:, :half], zero16], axis=1),
                                 jnp.concatenate([zero16, x[:, half:]], axis=1)], axis=0) for x in ps]

    bmp = [jnp.concatenate([x[:half, :half], x[half:, half:]], axis=1) for x in bm]
    in8 = pblk(SUBLANES)
    x0 = [jnp.where(in8, -x, 0.0) for x in bmp]
    tp = [(pr == pc).astype(F32) + x for x in x0]
    x16 = b16(x0)
    p16 = b16(mm(x16, unpack(x16)))
    pf16 = unpack(p16)
    tp = [t + d for t, d in zip(tp, mm(b16(tp), pf16))]
    pf16 = unpack(b16(mm(p16, pf16)))
    tp = [t + d for t, d in zip(tp, mm(b16(tp), pf16))]
    s = SUBLANES
    while s < half:
        m = jnp.logical_and(pblk(2 * s), jnp.logical_not(pblk(s)))
        off16 = unpack(b16([jnp.where(m, x, 0.0) for x in bmp]))
        tp16 = b16(tp)
        tp = [t - d for t, d in zip(tp, mm(b16(mm(tp16, off16)), unpack(tp16)))]
        s *= 2
    ta16 = b16([t[:, :half] for t in tp])
    tb16 = b16([t[:, half:] for t in tp])
    tc16 = b16([-d for d in mm(b16(mm(tb16, b16([x[half:, :half] for x in bm]))), ta16)])
    t16 = [jnp.concatenate([jnp.concatenate([a, zero16], axis=1), jnp.concatenate([c, b], axis=1)], axis=0)
           for a, b, c in zip(ta16, tb16, tc16)]

    q = [prepared(0, h) for h in heads]
    qk = [lax.dot_general(x, y, nt, preferred_element_type=F32) for x, y in zip(b16(q), k16)]
    attn16 = b16([qk[h] * decay[h] for h in heads])
    v = [prepared(2, h) for h in heads]
    sol = mm(t16, [jnp.concatenate([(v[h] * bcol[h]).astype(BF16), (kb[h] * eg[h]).astype(BF16)], axis=1)
                   for h in heads])
    value = [x[:, :HEAD_DIM] for x in sol]
    kcd16 = b16([x[:, HEAD_DIM:] for x in sol])
    st = [s_ref[0, h] for h in heads]
    st16 = b16(st)
    u16 = b16([value[h] - d for h, d in zip(heads, mm(kcd16, st16))])
    o = [a + c for a, c in zip(mm(b16([q[h] * eg[h] for h in heads]), st16), mm(attn16, u16))]
    glast = [gcol[h][C - 1:C, :] for h in heads]
    kd16 = b16([k[h] * jnp.exp(glast[h] - gcol[h]) for h in heads])
    ku = [lax.dot_general(x, y, tn, preferred_element_type=F32) for x, y in zip(kd16, u16)]
    for h in heads:
        s_ref[0, h] = st[h] * jnp.exp(glast[h]) + ku[h]
        on = o[h] * lax.rsqrt(jnp.mean(o[h] * o[h], axis=-1, keepdims=True) + RMS_EPS) * gon
        o_ref[:, hs[h]] = (on * _silu(z_ref[:, hs[h]])).astype(o_ref.dtype)


def _gdn_scan_prompt(proj, w_conv, beta, gcum, g_onorm, o_sample, Bp, T, H):
    Mp = Bp * T
    n_taps = w_conv.shape[0]
    z_col0 = 3 * H * HEAD_DIM
    assert n_taps - 1 <= SUBLANES
    Bs = o_sample.shape[0]
    hb = _pick_tile(H, 8, 8) if H % 8 == 0 else H
    tt = CHUNK
    assert T % CHUNK == 0 and (hb % 8 == 0 or hb == H) and Bs <= tt
    nhg, ntt = H // hb, T // tt
    n_chunks = Bp * ntt
    cw = hb * HEAD_DIM
    assert z_col0 % cw == 0
    beta_g = beta.reshape(Mp, nhg, hb).transpose(1, 0, 2)
    gcol_g = gcum.reshape(Mp, nhg, hb).transpose(1, 0, 2)
    grow_g = gcum.reshape(Mp // CHUNK, CHUNK, H).transpose(0, 2, 1)
    body = functools.partial(_gdn_scan_body, hb=hb, chunks_per_seq=ntt, n_chunks=n_chunks, n_taps=n_taps)
    row = lambda s: jnp.minimum(s, n_chunks - 1)
    before = lambda s: jnp.maximum(row(s) * (tt // SUBLANES) - 1, 0)
    blk = (4 * _nbytes((tt, cw), F32) + 2 * _nbytes((tt, LANES), F32) + _nbytes((hb, LANES), F32)
           + 6 * _nbytes((SUBLANES, cw), F32)
           + _nbytes((tt, cw), BF16) + _nbytes((Bs, cw), BF16) + _nbytes((hb, HEAD_DIM, HEAD_DIM), F32))
    temporaries = 8 * hb * _nbytes((CHUNK, CHUNK), F32)
    return pl.pallas_call(
        body,
        grid=(nhg, n_chunks + 1),
        in_specs=[
            pl.BlockSpec((tt, cw), lambda h, s: (row(s), h)),
            pl.BlockSpec((tt, cw), lambda h, s: (row(s), nhg + h)),
            pl.BlockSpec((tt, cw), lambda h, s: (row(s), 2 * nhg + h)),
            pl.BlockSpec((SUBLANES, cw), lambda h, s: (before(s), h)),
            pl.BlockSpec((SUBLANES, cw), lambda h, s: (before(s), nhg + h)),
            pl.BlockSpec((SUBLANES, cw), lambda h, s: (before(s), 2 * nhg + h)),
            pl.BlockSpec((n_taps, cw), lambda h, s: (0, h)),
            pl.BlockSpec((n_taps, cw), lambda h, s: (0, nhg + h)),
            pl.BlockSpec((n_taps, cw), lambda h, s: (0, 2 * nhg + h)),
            pl.BlockSpec((tt, cw), lambda h, s: (row(s), z_col0 // cw + h)),
            pl.BlockSpec((1, tt, hb), lambda h, s: (h, row(s), 0)),
            pl.BlockSpec((1, tt, hb), lambda h, s: (h, row(s), 0)),
            pl.BlockSpec((1, hb, CHUNK), lambda h, s: (row(s), h, 0)),
            pl.BlockSpec((1, HEAD_DIM), lambda h, s: (0, 0)),
            pl.BlockSpec((Bs, cw), lambda h, s: (0, h)),
        ],
        out_specs=[
            pl.BlockSpec((tt, cw), lambda h, s: (s, h)),
            pl.BlockSpec((1, hb, HEAD_DIM, HEAD_DIM), lambda h, s: (row(s) // ntt, h, 0, 0)),
        ],
        out_shape=[jax.ShapeDtypeStruct((Mp + Bs, H * HEAD_DIM), BF16),
                   jax.ShapeDtypeStruct((Bp, H, HEAD_DIM, HEAD_DIM), F32)],
        compiler_params=_params(("parallel", "arbitrary"), blk, temporaries),
        name="gdn_scan_prompt",
    )(proj, proj, proj, proj, proj, proj, w_conv, w_conv, w_conv, proj, beta_g, gcol_g, grow_g,
      g_onorm.reshape(1, HEAD_DIM), o_sample)


def _gdn_step_body(a_ref, beta_ref, s_ref, kt_ref, qt_ref, v_ref, z_ref, gon_ref, so_ref, o_ref, orow_ref, *, n_heads):
    b = pl.program_id(0)
    for h in range(n_heads):
        st = s_ref[0, h]
        a = a_ref[b, h]
        bt = beta_ref[b, h]
        kc = jnp.broadcast_to(kt_ref[0, :, h:h + 1], st.shape)
        qc = jnp.broadcast_to(qt_ref[0, :, h:h + 1], st.shape)
        sk = jnp.sum(st * kc, axis=0, keepdims=True)
        w = v_ref[0, h:h + 1, :] - a * sk
        sn = a * st + (bt * kc) * w
        so_ref[0, h] = sn
        orow_ref[h:h + 1, :] = jnp.sum(sn * qc, axis=0, keepdims=True)
    o = orow_ref[...]
    on = o * lax.rsqrt(jnp.mean(o * o, axis=-1, keepdims=True) + RMS_EPS) * gon_ref[...]
    o_ref[0] = (on * _silu(z_ref[0])).astype(o_ref.dtype)


def _gdn_step_sample(qkv_s, z_s, beta_s, a_s, state, g_onorm):
    Bs, H = beta_s.shape
    hd = H * HEAD_DIM
    q3 = qkv_s[:, :hd].reshape(Bs, H, HEAD_DIM).transpose(0, 2, 1)
    k3 = qkv_s[:, hd:2 * hd].reshape(Bs, H, HEAD_DIM).transpose(0, 2, 1)
    v3 = qkv_s[:, 2 * hd:].reshape(Bs, H, HEAD_DIM)
    z3 = z_s.reshape(Bs, H, HEAD_DIM)
    smem = pl.BlockSpec(memory_space=pltpu.SMEM)
    body = functools.partial(_gdn_step_body, n_heads=H)
    blk = 2 * _nbytes((H, HEAD_DIM, HEAD_DIM), F32) + 6 * _nbytes((HEAD_DIM, LANES), F32)
    so, o = pl.pallas_call(
        body,
        grid=(Bs,),
        in_specs=[smem, smem,
                  pl.BlockSpec((1, H, HEAD_DIM, HEAD_DIM), lambda b: (b, 0, 0, 0)),
                  pl.BlockSpec((1, HEAD_DIM, H), lambda b: (b, 0, 0)),
                  pl.BlockSpec((1, HEAD_DIM, H), lambda b: (b, 0, 0)),
                  pl.BlockSpec((1, H, HEAD_DIM), lambda b: (b, 0, 0)),
                  pl.BlockSpec((1, H, HEAD_DIM), lambda b: (b, 0, 0)),
                  pl.BlockSpec((1, HEAD_DIM), lambda b: (0, 0))],
        out_specs=[pl.BlockSpec((1, H, HEAD_DIM, HEAD_DIM), lambda b: (b, 0, 0, 0)),
                   pl.BlockSpec((1, H, HEAD_DIM), lambda b: (b, 0, 0))],
        out_shape=[jax.ShapeDtypeStruct((Bs, H, HEAD_DIM, HEAD_DIM), F32),
                   jax.ShapeDtypeStruct((Bs, H, HEAD_DIM), BF16)],
        scratch_shapes=[pltpu.VMEM((H, HEAD_DIM), F32)],
        compiler_params=_params(("arbitrary",), blk),
        name="gdn_step_sample",
    )(a_s, beta_s, state, k3, q3, v3, z3, g_onorm.reshape(1, HEAD_DIM))
    return o.reshape(Bs, hd), so


def _seq_tails(a, Bp, T, n, ncols):
    return jnp.stack([lax.slice(a, (b * T + T - n, 0), ((b + 1) * T, ncols)) for b in range(Bp)])


def _wcol_bytes(*ws):
    return sum(w.shape[1] * w.dtype.itemsize for w in ws)


def _conformer_layer(h, xn, g_next, Mp, Bp, T, Bs, state, j, w_pw1, w_dw, b_dw, ln_g, ln_b, w_pw2):
    M, D = h.shape
    Dc = w_pw1.shape[2] // 2
    tm, tn = _mm_tiles(M, D, Dc, 2 * _wcol_bytes(w_pw1))
    glu = _matmul(_mm_glu_body, "mm_glu", M, Dc, tm, tn,
                  _xn_specs(xn, tm) + [_w_spec(w_pw1, j, tn), _w_spec(w_pw1, j, tn, Dc // tn)], F32)
    c_s, new_s = _conf_conv_sample(glu, Mp, Bs, jnp.swapaxes(state, 0, 1), w_dw, b_dw, ln_g, ln_b)
    new_s = jnp.swapaxes(new_s, 0, 1)
    c = _conf_conv_prompt(glu, Bp, T, c_s, w_dw, b_dw, ln_g, ln_b)
    tm, tn = _mm_tiles(M, Dc, D, _wcol_bytes(w_pw2))
    h, xn = _matmul(_mm_res_body, "mm_pw2", M, D, tm, tn,
                    [_x_spec(c, tm), _w_spec(w_pw2, j, tn), _r_spec(h, tm, tn), _g_spec(g_next, tn)], F32, True)
    new_p = _seq_tails(glu, Bp, T, w_dw.shape[0] - 1, Dc)
    return h, xn, new_p, new_s


def _gdn_layer(h, xn, g_next, Mp, Bp, T, Bs, conv_state, gdn_state, j, w_in, w_conv, a_log, dt_bias, g_onorm, w_out):
    M, D = h.shape
    H = a_log.shape[0]
    hd = H * HEAD_DIM
    qkv_dim = 3 * hd
    n_main = qkv_dim + hd
    tm, tn = _mm_tiles(M, D, n_main, _wcol_bytes(w_in))
    w_in_t = jnp.swapaxes(w_in, 1, 2)
    wt_spec = lambda rows, row_block0: (w_in_t, (None, rows, D), lambda i, c: (j, c + row_block0, 0))
    proj = _matmul(_mm_plain_wt_body, "mm_gdn_in", M, n_main, tm, tn, _xn_specs(xn, tm) + [wt_spec(tn, 0)], F32)
    assert n_main % (2 * H) == 0
    logits = _matmul(_mm_plain_wt_body, "mm_gdn_gates", M, 2 * H, tm, 2 * H,
                     _xn_specs(xn, tm) + [wt_spec(2 * H, n_main // (2 * H))], F32)

    beta_p, gcum_p = _gates(logits, a_log, dt_bias, 0, Mp, CHUNK)
    beta_s, a_s = _gates(logits, a_log, dt_bias, Mp, Bs, 1)

    qkv_s, new_conv_s = _qkv_prep_sample(proj, Mp, Bs, qkv_dim, jnp.swapaxes(conv_state, 0, 1), w_conv)
    new_conv_s = jnp.swapaxes(new_conv_s, 0, 1)
    o_s, s_s = _gdn_step_sample(qkv_s, lax.slice(proj, (Mp, qkv_dim), (M, n_main)), beta_s, a_s, gdn_state, g_onorm)

    o, s_p = _gdn_scan_prompt(proj, w_conv, beta_p, gcum_p, g_onorm, o_s, Bp, T, H)
    tm, tn = _mm_tiles(M, hd, D, _wcol_bytes(w_out))
    h, xn = _matmul(_mm_res_body, "mm_gdn_out", M, D, tm, tn,
                    [_x_spec(o, tm), _w_spec(w_out, j, tn), _r_spec(h, tm, tn), _g_spec(g_next, tn)], F32, True)
    new_conv_p = _seq_tails(proj, Bp, T, w_conv.shape[0] - 1, qkv_dim)
    return h, xn, new_conv_p, new_conv_s, s_p, s_s


def _ffn_ple(h, xn, p_all, i, w_gate, w_up, w_down, g_ple, w_pg, w_pp, g_next):
    M, D = h.shape
    F = w_gate.shape[2]
    tm, tn = _mm_tiles(M, D, F, _wcol_bytes(w_gate, w_up))
    up_operands = _xn_specs(xn, tm) + [_w_spec(w_gate, i, tn), _w_spec(w_up, i, tn)]
    n_steps = (M // tm) * (F // tn)
    slab = F // n_steps
    if F % n_steps == 0 and slab % 16 == 0:
        nj = F // tn
        act, w_down16 = _matmul(
            _mm_swiglu_cast_body, "mm_ffn_up", M, F, tm, tn,
            up_operands + [(w_down, (None, slab, D), lambda r, c: (i, r * nj + c, 0))], BF16,
            side_out=((F, D), BF16, (slab, D), lambda r, c: (r * nj + c, 0)))
    else:
        act = _matmul(_mm_swiglu_body, "mm_ffn_up", M, F, tm, tn, up_operands, BF16)
        w_down16 = w_down[i].astype(BF16)
    w_down16 = w_down16[None]
    tm, tn = _mm_tiles(M, F, D, _wcol_bytes(w_down16))
    h, xn = _matmul(_mm_res_body, "mm_ffn_down", M, D, tm, tn,
                    [_x_spec(act, tm), _w_spec(w_down16, 0, tn), _r_spec(h, tm, tn), _g_spec(g_ple, tn)], F32, True)
    tm, tn = _mm_tiles(M, D, D, _wcol_bytes(w_pg, w_pp))
    p_spec = (p_all, (None, tm, p_all.shape[2]), lambda r, c: (i, r, 0))
    operands = _xn_specs(xn, tm) + [p_spec, _w_spec(w_pg, i, tn), _w_spec(w_pp, i, tn), _r_spec(h, tm, tn)]
    if g_next is None:
        return _matmul(_mm_ple_last_body, "mm_ple", M, D, tm, tn, operands, F32), None
    return _matmul(_mm_ple_body, "mm_ple", M, D, tm, tn, operands + [_g_spec(g_next, tn)], F32, True)


def kernel(x_prompt, x_sample, p_prompt, p_sample, state_conv_conformer, state_conv_qkv, state_gdn, g_mix, g_ffn, g_ple, g_final, conf_w_pw1, conf_w_dw, conf_b_dw, conf_ln_g, conf_ln_b, conf_w_pw2, gdn_w_in, gdn_w_conv, gdn_a_log, gdn_dt_bias, gdn_g_onorm, gdn_w_out, ffn_w_gate, ffn_w_up, ffn_w_down, ple_w_gate, ple_w_proj):
    Bp, T, D = x_prompt.shape
    Bs, Ts, _ = x_sample.shape
    assert Ts == 1
    depth = g_mix.shape[0]
    Mp = Bp * T
    M = Mp + Bs
    h, xn = _embed(x_prompt.reshape(Mp, D), x_sample.reshape(Bs, D), g_mix[0])
    p_all = jnp.concatenate([p_prompt.reshape(depth, Mp, -1), p_sample.reshape(depth, Bs, -1)], axis=1).astype(BF16)
    conf_p, conf_s, qkv_p, qkv_s, gdn_p, gdn_s = [], [], [], [], [], []
    for i in range(depth):
        j = i // 2
        if i % 2 == 0:
            h, xn, np_, ns_ = _conformer_layer(h, xn, g_ffn[i], Mp, Bp, T, Bs, state_conv_conformer[j], j, conf_w_pw1,
                                               conf_w_dw[j], conf_b_dw[j], conf_ln_g[j], conf_ln_b[j], conf_w_pw2)
            conf_p.append(np_)
            conf_s.append(ns_)
        else:
            h, xn, cp_, cs_, sp_, ss_ = _gdn_layer(h, xn, g_ffn[i], Mp, Bp, T, Bs, state_conv_qkv[j], state_gdn[j], j,
                                                   gdn_w_in, gdn_w_conv[j], gdn_a_log[j], gdn_dt_bias[j],
                                                   gdn_g_onorm[j], gdn_w_out)
            qkv_p.append(cp_)
            qkv_s.append(cs_)
            gdn_p.append(sp_)
            gdn_s.append(ss_)
        h, xn = _ffn_ple(h, xn, p_all, i, ffn_w_gate, ffn_w_up, ffn_w_down, g_ple[i], ple_w_gate, ple_w_proj,
                         g_mix[i + 1] if i + 1 < depth else None)

    y_p = _rmsnorm(h, g_final, F32, 0, Mp).reshape(Bp, T, D)
    y_s = _rmsnorm(h, g_final, F32, Mp, Bs).reshape(Bs, 1, D)
    return (y_p, y_s, jnp.stack(conf_p), jnp.stack(qkv_p), jnp.stack(gdn_p),
            jnp.stack(conf_s), jnp.stack(qkv_s), jnp.stack(gdn_s))
```

```python
import functools

import jax
import jax.numpy as jnp
from jax import lax
from jax.experimental import pallas as pl
from jax.experimental.pallas import tpu as pltpu

F32 = jnp.float32
BF16 = jnp.bfloat16

RMS_EPS = 1e-6
LN_EPS = 1e-5
L2_EPS = 1e-6
CHUNK = 256
HEAD_DIM = 128
LANES = 128
SUBLANES = 8
V7X_VMEM_LIMIT_CAP = 60000 * 1024
VMEM_INTERNAL_SCRATCH = 8 * 2**20


def _pick_tile(n, cap, mult):
    best = None
    for d in range(mult, min(n, cap) + 1, mult):
        if n % d == 0:
            best = d
    return n if best is None else best


def _params(sem, block_bytes, scratch_bytes=0):
    need = 2 * block_bytes + scratch_bytes + VMEM_INTERNAL_SCRATCH
    return pltpu.CompilerParams(dimension_semantics=sem,
                                vmem_limit_bytes=int(min(max(need, 32 * 2**20), V7X_VMEM_LIMIT_CAP)))


def _nbytes(shape, dtype):
    n = 1
    for s in shape:
        n *= 1 if s is None else s
    return n * jnp.dtype(dtype).itemsize


def _bdot(a, b):
    return jnp.dot(a.astype(BF16), b.astype(BF16), preferred_element_type=F32)


def _silu(x):
    return x * jax.nn.sigmoid(x)


def _rms_body(x_ref, g_ref, o_ref):
    x = x_ref[...]
    y = x * lax.rsqrt(jnp.mean(x * x, axis=-1, keepdims=True) + RMS_EPS)
    o_ref[...] = (y * g_ref[...]).astype(o_ref.dtype)


def _rmsnorm(x, g, out_dtype, row0=0, rows=None):
    M, D = x.shape
    rows = M if rows is None else rows
    tm = _pick_tile(rows, 320, 16)
    assert row0 % tm == 0
    blk = _nbytes((tm, D), F32) + _nbytes((tm, D), out_dtype)
    return pl.pallas_call(
        _rms_body,
        grid=(rows // tm,),
        in_specs=[pl.BlockSpec((tm, D), lambda i: (i + row0 // tm, 0)),
                  pl.BlockSpec((1, D), lambda i: (0, 0))],
        out_specs=pl.BlockSpec((tm, D), lambda i: (i, 0)),
        out_shape=jax.ShapeDtypeStruct((rows, D), out_dtype),
        compiler_params=_params(("parallel",), blk),
        name="rmsnorm",
    )(x, g.reshape(1, D))


def _embed_body(xp_ref, xs_ref, g_ref, h_ref, xg_ref, ss_ref, *, n_prompt_tiles):
    def emit(x):
        h_ref[...] = x
        xg_ref[...] = (x * g_ref[...]).astype(xg_ref.dtype)
        ss_ref[...] = jnp.sum(x * x, axis=-1, keepdims=True)

    @pl.when(pl.program_id(0) < n_prompt_tiles)
    def _():
        emit(xp_ref[...])

    @pl.when(pl.program_id(0) >= n_prompt_tiles)
    def _():
        emit(xs_ref[...])


def _embed(xp, xs, g):
    (Mp, D), Bs = xp.shape, xs.shape[0]
    tr = Bs
    assert Mp % tr == 0 and tr % SUBLANES == 0
    npt = Mp // tr
    body = functools.partial(_embed_body, n_prompt_tiles=npt)
    blk = 3 * _nbytes((tr, D), F32) + _nbytes((tr, D), BF16) + _nbytes((tr, LANES), F32)
    h, xg, ss = pl.pallas_call(
        body,
        grid=(npt + 1,),
        in_specs=[pl.BlockSpec((tr, D), lambda i: (jnp.minimum(i, npt - 1), 0)),
                  pl.BlockSpec((tr, D), lambda i: (0, 0)),
                  pl.BlockSpec((1, D), lambda i: (0, 0))],
        out_specs=[pl.BlockSpec((tr, D), lambda i: (i, 0)), pl.BlockSpec((tr, D), lambda i: (i, 0)),
                   pl.BlockSpec((tr, 1), lambda i: (i, 0))],
        out_shape=[jax.ShapeDtypeStruct((Mp + Bs, D), F32), jax.ShapeDtypeStruct((Mp + Bs, D), BF16),
                   jax.ShapeDtypeStruct((Mp + Bs, 1), F32)],
        compiler_params=_params(("parallel",), blk),
        name="embed_norm",
    )(xp, xs, g.reshape(1, D))
    return h, (xg, ss)


def _wdot(x_ref, w_ref):
    return jnp.dot(x_ref[...], w_ref[...].astype(BF16), preferred_element_type=F32)


def _row_scale(ss_ref, width):
    return lax.rsqrt(ss_ref[...] * (1.0 / width) + RMS_EPS)


def _emit_normalised(val, g_ref, o_ref, xg_ref, ss_ref):
    o_ref[...] = val
    xg_ref[...] = (val * g_ref[...]).astype(xg_ref.dtype)
    part = jnp.sum(val * val, axis=-1, keepdims=True)
    j = pl.program_id(1)

    @pl.when(j == 0)
    def _():
        ss_ref[...] = part

    @pl.when(j > 0)
    def _():
        ss_ref[...] += part


def _mm_glu_body(x_ref, ss_ref, wa_ref, wg_ref, o_ref):
    r = _row_scale(ss_ref, x_ref.shape[1])
    o_ref[...] = ((_wdot(x_ref, wa_ref) * r) * jax.nn.sigmoid(_wdot(x_ref, wg_ref) * r)).astype(o_ref.dtype)


def _mm_swiglu_body(x_ref, ss_ref, wg_ref, wu_ref, o_ref):
    r = _row_scale(ss_ref, x_ref.shape[1])
    o_ref[...] = (_silu(_wdot(x_ref, wg_ref) * r) * (_wdot(x_ref, wu_ref) * r)).astype(o_ref.dtype)


def _mm_swiglu_cast_body(x_ref, ss_ref, wg_ref, wu_ref, wd_ref, o_ref, wd16_ref):
    _mm_swiglu_body(x_ref, ss_ref, wg_ref, wu_ref, o_ref)
    wd16_ref[...] = wd_ref[...].astype(wd16_ref.dtype)


def _mm_plain_body(x_ref, ss_ref, w_ref, o_ref):
    o_ref[...] = (_wdot(x_ref, w_ref) * _row_scale(ss_ref, x_ref.shape[1])).astype(o_ref.dtype)


def _mm_plain_wt_body(x_ref, ss_ref, wt_ref, o_ref):
    d = lax.dot_general(x_ref[...], wt_ref[...].astype(BF16), (((1,), (1,)), ((), ())), preferred_element_type=F32)
    o_ref[...] = (d * _row_scale(ss_ref, x_ref.shape[1])).astype(o_ref.dtype)


def _mm_res_body(x_ref, w_ref, r_ref, g_ref, o_ref, xg_ref, sso_ref):
    _emit_normalised(r_ref[...] + _wdot(x_ref, w_ref), g_ref, o_ref, xg_ref, sso_ref)


def _ple_value(x_ref, ss_ref, p_ref, wg_ref, wp_ref, r_ref):
    gate = jax.nn.sigmoid(_wdot(x_ref, wg_ref) * _row_scale(ss_ref, x_ref.shape[1]))
    return r_ref[...] + gate * _wdot(p_ref, wp_ref)


def _mm_ple_body(x_ref, ss_ref, p_ref, wg_ref, wp_ref, r_ref, g_ref, o_ref, xg_ref, sso_ref):
    _emit_normalised(_ple_value(x_ref, ss_ref, p_ref, wg_ref, wp_ref, r_ref), g_ref, o_ref, xg_ref, sso_ref)


def _mm_ple_last_body(x_ref, ss_ref, p_ref, wg_ref, wp_ref, r_ref, o_ref):
    o_ref[...] = _ple_value(x_ref, ss_ref, p_ref, wg_ref, wp_ref, r_ref)


def _matmul(body, name, M, N, tm, tn, operands, out_dtype, emit_normalised=False, side_out=None):
    blk = sum(_nbytes(bs, a.dtype) for a, bs, _ in operands) + _nbytes((tm, tn), out_dtype)
    out_specs = [pl.BlockSpec((tm, tn), lambda i, j: (i, j))]
    out_shape = [jax.ShapeDtypeStruct((M, N), out_dtype)]
    if emit_normalised:
        out_specs += [pl.BlockSpec((tm, tn), lambda i, j: (i, j)), pl.BlockSpec((tm, 1), lambda i, j: (i, 0))]
        out_shape += [jax.ShapeDtypeStruct((M, N), BF16), jax.ShapeDtypeStruct((M, 1), F32)]
        blk += _nbytes((tm, tn), BF16) + _nbytes((tm, LANES), F32)
    if side_out is not None:
        out_specs.append(pl.BlockSpec(side_out[2], side_out[3]))
        out_shape.append(jax.ShapeDtypeStruct(side_out[0], side_out[1]))
        blk += _nbytes(side_out[2], side_out[1])
    outs = pl.pallas_call(
        body,
        grid=(M // tm, N // tn),
        in_specs=[pl.BlockSpec(bs, im) for _, bs, im in operands],
        out_specs=out_specs,
        out_shape=out_shape,
        compiler_params=_params(("parallel", "arbitrary"), blk),
        name=name,
    )(*[a for a, _, _ in operands])
    main = (outs[0], (outs[1], outs[2])) if emit_normalised else outs[0]
    return main if side_out is None else (main, outs[-1])


def _x_spec(x, tm):
    return (x, (tm, x.shape[1]), lambda i, j: (i, 0))


def _xn_specs(xn, tm):
    xg, ss = xn
    return [_x_spec(xg, tm), (ss, (tm, 1), lambda i, j: (i, 0))]


def _w_spec(w, layer, tn, col_block0=0):
    return (w, (None, w.shape[1], tn), lambda i, j: (layer, 0, j + col_block0))


def _r_spec(r, tm, tn):
    return (r, (tm, tn), lambda i, j: (i, j))


def _g_spec(g, tn):
    return (g.reshape(1, -1), (1, tn), lambda i, j: (0, j))


def _mm_tiles(M, K, N, w_bytes_per_col):
    tn = _pick_tile(N, 512 if 512 * w_bytes_per_col <= 9 * 2**20 else 256, LANES)
    tm_cap = 1040 if K <= 4096 else 640
    return _pick_tile(M, tm_cap, 16), tn


def _conv_blocks(load_blk, w_row, n_taps, n_out, width):
    qmax = (n_taps - 1) // SUBLANES
    rows = lax.broadcasted_iota(jnp.int32, (SUBLANES, width), 0)
    x = {k: load_blk(k) for k in range(-(qmax + 1), n_out)}
    acc = [None] * n_out
    for r in range(min(SUBLANES, n_taps)):
        if r == 0:
            y = x
        else:
            rot = {k: pltpu.roll(v, r, 0) for k, v in x.items()}
            y = {k: jnp.where(rows >= r, rot[k], rot[k - 1]) for k in range(-qmax, n_out)}
        for q in range(qmax + 1):
            d = SUBLANES * q + r
            if d >= n_taps:
                continue
            w = w_row(d)
            for j in range(n_out):
                t = y[j - q] * w
                acc[j] = t if acc[j] is None else acc[j] + t
    return acc


CONF_HALO = 32
CONV_STRIP = 8
CONV_ROWS_CAP = 256


def _conf_conv_body(x_ref, halo_ref, cs_ref, w_ref, b_ref, lg_ref, lb_ref, o_ref, xe_ref, y_ref, *,
                    tb, ncb, n_taps, tiles_per_seq, n_tiles):
    step = pl.program_id(0)

    @pl.when(step == n_tiles)
    def _():
        o_ref[0:cs_ref.shape[0], :] = cs_ref[...].astype(o_ref.dtype)

    @pl.when(step < n_tiles)
    def _():
        _conf_conv_tile(x_ref, halo_ref, w_ref, b_ref, lg_ref, lb_ref, o_ref, xe_ref, y_ref,
                        first=step % tiles_per_seq == 0, tb=tb, ncb=ncb, n_taps=n_taps)


def _conf_conv_tile(x_ref, halo_ref, w_ref, b_ref, lg_ref, lb_ref, o_ref, xe_ref, y_ref, *, first, tb, ncb, n_taps):
    for cb in range(ncb):
        sl = slice(cb * LANES, (cb + 1) * LANES)
        xe_ref[cb, CONF_HALO:CONF_HALO + tb, :] = x_ref[:, sl]
        xe_ref[cb, 0:CONF_HALO, :] = jnp.where(first, 0.0, halo_ref[:, sl])

    strip_rows = SUBLANES * CONV_STRIP
    n_strips = tb // strip_rows

    def col_loop(cb, carry):
        def strip_loop(s, carry2):
            r0 = pl.multiple_of(s * strip_rows, strip_rows)
            acc = _conv_blocks(
                lambda k: xe_ref[cb, pl.ds(r0 + (CONF_HALO + SUBLANES * k), SUBLANES), :],
                lambda d: w_ref[cb, pl.ds(n_taps - 1 - d, 1), :],
                n_taps, CONV_STRIP, LANES)
            bias = b_ref[cb]
            for j in range(CONV_STRIP):
                y_ref[cb, pl.ds(r0 + SUBLANES * j, SUBLANES), :] = acc[j] + bias
            return carry2
        return lax.fori_loop(0, n_strips, strip_loop, carry)

    lax.fori_loop(0, ncb, col_loop, 0)

    ln_rows = 32
    inv_c = 1.0 / (ncb * LANES)

    def ln_loop(s, carry):
        r0 = pl.multiple_of(s * ln_rows, ln_rows)
        y = y_ref[:, pl.ds(r0, ln_rows), :]
        mu = jnp.sum(jnp.sum(y, axis=0), axis=-1, keepdims=True) * inv_c
        yc = y - mu[None]
        var = jnp.sum(jnp.sum(yc * yc, axis=0), axis=-1, keepdims=True) * inv_c
        rstd = lax.rsqrt(var + LN_EPS)
        for cb in range(ncb):
            t = yc[cb] * rstd * lg_ref[cb] + lb_ref[cb]
            o_ref[pl.ds(r0, ln_rows), cb * LANES:(cb + 1) * LANES] = _silu(t).astype(o_ref.dtype)
        return carry

    lax.fori_loop(0, tb // ln_rows, ln_loop, 0)


def _col_major(v, pad_rows=None):
    R, C = v.shape
    out = v.reshape(R, C // LANES, LANES).transpose(1, 0, 2)
    if pad_rows is not None and pad_rows > R:
        out = jnp.pad(out, ((0, 0), (0, pad_rows - R), (0, 0)))
    return out


def _conf_conv_prompt(glu, Bp, T, c_sample, w_dw, b_dw, ln_g, ln_b):
    Dc = glu.shape[1]
    Bs = c_sample.shape[0]
    n_taps = w_dw.shape[0]
    assert n_taps - 1 <= CONF_HALO and T >= CONF_HALO
    tb = _pick_tile(T, CONV_ROWS_CAP, SUBLANES * CONV_STRIP)
    assert Bs <= tb
    ncb = Dc // LANES
    n_tiles = Bp * T // tb
    body = functools.partial(_conf_conv_body, tb=tb, ncb=ncb, n_taps=n_taps, tiles_per_seq=T // tb, n_tiles=n_tiles)
    hb = tb // CONF_HALO
    blk = (_nbytes((tb, Dc), F32) + _nbytes((CONF_HALO, Dc), F32) + _nbytes((tb, Dc), BF16) + _nbytes((Bs, Dc), F32)
           + _nbytes((ncb, 32 + 3, LANES), F32))
    scratch = _nbytes((ncb, tb + CONF_HALO, LANES), F32) + _nbytes((ncb, tb, LANES), F32)
    tile = lambda s: jnp.minimum(s, n_tiles - 1)
    const3 = lambda s: (0, 0, 0)
    return pl.pallas_call(
        body,
        grid=(n_tiles + 1,),
        in_specs=[
            pl.BlockSpec((tb, Dc), lambda s: (tile(s), 0)),
            pl.BlockSpec((CONF_HALO, Dc), lambda s: (jnp.maximum(tile(s) * hb - 1, 0), 0)),
            pl.BlockSpec((Bs, Dc), lambda s: (0, 0)),
            pl.BlockSpec((ncb, 32, LANES), const3),
            pl.BlockSpec((ncb, 1, LANES), const3),
            pl.BlockSpec((ncb, 1, LANES), const3),
            pl.BlockSpec((ncb, 1, LANES), const3),
        ],
        out_specs=pl.BlockSpec((tb, Dc), lambda s: (s, 0)),
        out_shape=jax.ShapeDtypeStruct((Bp * T + Bs, Dc), BF16),
        scratch_shapes=[pltpu.VMEM((ncb, tb + CONF_HALO, LANES), F32), pltpu.VMEM((ncb, tb, LANES), F32)],
        compiler_params=_params(("arbitrary",), blk, scratch),
        name="conf_conv_prompt",
    )(glu, glu, c_sample, _col_major(w_dw, 32), _col_major(b_dw[None]), _col_major(ln_g[None]), _col_major(ln_b[None]))


def _conf_conv_sample_body(st_ref, x_ref, w_ref, b_ref, lg_ref, lb_ref, o_ref, so_ref, acc_ref, *, n_prev):
    w = pl.program_id(0)

    @pl.when(w == 0)
    def _():
        acc_ref[...] = x_ref[...] * w_ref[pl.ds(n_prev, 1), :]

    @pl.when(w < n_prev)
    def _():
        st = st_ref[...]
        acc_ref[...] += st * w_ref[pl.ds(w, 1), :]
        so_ref[...] = st

    @pl.when(w == n_prev)
    def _():
        so_ref[...] = x_ref[...]
        y = acc_ref[...] + b_ref[...]
        mu = jnp.mean(y, axis=-1, keepdims=True)
        yc = y - mu
        rstd = lax.rsqrt(jnp.mean(yc * yc, axis=-1, keepdims=True) + LN_EPS)
        o_ref[...] = _silu(yc * rstd * lg_ref[...] + lb_ref[...])


def _conf_conv_sample(glu, Mp, Bs, state_t, w_dw, b_dw, ln_g, ln_b):
    Dc = glu.shape[1]
    n_taps = w_dw.shape[0]
    n_prev = n_taps - 1
    assert Mp % Bs == 0
    blk = 4 * _nbytes((Bs, Dc), F32) + _nbytes((n_taps + 3 * SUBLANES, Dc), F32)
    row = lambda w: (0, 0)
    body = functools.partial(_conf_conv_sample_body, n_prev=n_prev)
    return pl.pallas_call(
        body,
        grid=(n_prev + 1,),
        in_specs=[
            pl.BlockSpec((None, Bs, Dc), lambda w: (jnp.minimum(w, n_prev - 1), 0, 0)),
            pl.BlockSpec((Bs, Dc), lambda w: (Mp // Bs, 0)),
            pl.BlockSpec((n_taps, Dc), row),
            pl.BlockSpec((1, Dc), row), pl.BlockSpec((1, Dc), row), pl.BlockSpec((1, Dc), row),
        ],
        out_specs=[pl.BlockSpec((Bs, Dc), lambda w: (0, 0)),
                   pl.BlockSpec((None, Bs, Dc), lambda w: (jnp.maximum(w - 1, 0), 0, 0))],
        out_shape=[jax.ShapeDtypeStruct((Bs, Dc), F32), jax.ShapeDtypeStruct((n_prev, Bs, Dc), F32)],
        scratch_shapes=[pltpu.VMEM((Bs, Dc), F32)],
        compiler_params=_params(("arbitrary",), blk, _nbytes((Bs, Dc), F32)),
        name="conf_conv_sample",
    )(state_t, glu, w_dw, b_dw[None], ln_g[None], ln_b[None])


def _qkv_finish(y, kind):
    y = _silu(y)
    if kind == 2:
        return y
    fac = lax.rsqrt(jnp.sum(y * y, axis=-1, keepdims=True) + L2_EPS)
    return y * (fac * HEAD_DIM ** -0.5 if kind == 0 else fac)


def _for_each_kind(kind, fn):
    for k in range(3):
        pl.when(kind == k)(functools.partial(fn, k))


def _qkv_prep_sample_body(st_ref, x_ref, w_ref, o_ref, so_ref, *, n_prev, cw, groups_per_kind):
    kind = pl.program_id(0) // groups_per_kind
    x = x_ref[...]
    y = x * w_ref[n_prev:n_prev + 1, :]
    for w in range(n_prev):
        y = y + st_ref[w] * w_ref[w:w + 1, :]
    for w in range(1, n_prev):
        so_ref[w - 1] = st_ref[w]
    so_ref[n_prev - 1] = x

    def finish(static_kind):
        for cb in range(cw // LANES):
            sl = slice(cb * LANES, (cb + 1) * LANES)
            o_ref[:, sl] = _qkv_finish(y[:, sl], static_kind)

    _for_each_kind(kind, finish)


def _qkv_prep_sample(proj, Mp, Bs, qkv_dim, state_t, w_conv):
    n_prev = w_conv.shape[0] - 1
    cw = _pick_tile(qkv_dim // 3, 1024, LANES)
    ncg = qkv_dim // cw
    assert Mp % Bs == 0
    body = functools.partial(_qkv_prep_sample_body, n_prev=n_prev, cw=cw, groups_per_kind=ncg // 3)
    blk = 2 * _nbytes((n_prev, Bs, cw), F32) + 2 * _nbytes((Bs, cw), F32) + _nbytes((SUBLANES, cw), F32)
    return pl.pallas_call(
        body,
        grid=(ncg,),
        in_specs=[pl.BlockSpec((n_prev, Bs, cw), lambda c: (0, 0, c)),
                  pl.BlockSpec((Bs, cw), lambda c: (Mp // Bs, c)),
                  pl.BlockSpec((n_prev + 1, cw), lambda c: (0, c))],
        out_specs=[pl.BlockSpec((Bs, cw), lambda c: (0, c)), pl.BlockSpec((n_prev, Bs, cw), lambda c: (0, 0, c))],
        out_shape=[jax.ShapeDtypeStruct((Bs, qkv_dim), F32), jax.ShapeDtypeStruct((n_prev, Bs, qkv_dim), F32)],
        compiler_params=_params(("parallel",), blk),
        name="qkv_prep_sample",
    )(state_t, proj, w_conv)


def _gates_body(l_ref, alog_ref, dtb_ref, beta_ref, g_ref, *, n_heads, rows, chunk):
    lg = l_ref[...]
    beta_ref[...] = jax.nn.sigmoid(lg[:, :n_heads])
    g = -jnp.exp(alog_ref[...]) * jax.nn.softplus(lg[:, n_heads:] + dtb_ref[...])
    if chunk == 1:
        g_ref[...] = jnp.exp(g)
    else:
        tril = (lax.broadcasted_iota(jnp.int32, (chunk, chunk), 0)
                >= lax.broadcasted_iota(jnp.int32, (chunk, chunk), 1)).astype(F32)
        for c in range(rows // chunk):
            g_ref[c * chunk:(c + 1) * chunk, :] = jnp.dot(
                tril, g[c * chunk:(c + 1) * chunk, :], preferred_element_type=F32, precision=lax.Precision.HIGHEST)


def _gates(logits, a_log, dt_bias, row0, rows, chunk):
    H = logits.shape[1] // 2
    tr = _pick_tile(rows, 512, max(chunk, SUBLANES))
    assert row0 % tr == 0
    body = functools.partial(_gates_body, n_heads=H, rows=tr, chunk=chunk)
    return pl.pallas_call(
        body,
        grid=(rows // tr,),
        in_specs=[pl.BlockSpec((tr, 2 * H), lambda i: (i + row0 // tr, 0)),
                  pl.BlockSpec((1, H), lambda i: (0, 0)), pl.BlockSpec((1, H), lambda i: (0, 0))],
        out_specs=[pl.BlockSpec((tr, H), lambda i: (i, 0)), pl.BlockSpec((tr, H), lambda i: (i, 0))],
        out_shape=[jax.ShapeDtypeStruct((rows, H), F32), jax.ShapeDtypeStruct((rows, H), F32)],
        compiler_params=_params(("parallel",), 4 * _nbytes((tr, LANES), F32)),
        name="gdn_gates",
    )(logits, a_log.reshape(1, H), dt_bias.reshape(1, H))


def _gdn_scan_body(q_ref, k_ref, v_ref, qh_ref, kh_ref, vh_ref, wq_ref, wk_ref, wv_ref, z_ref, beta_ref, g_ref,
                   grow_ref, gon_ref, os_ref, o_ref, s_ref, *, hb, chunks_per_seq, n_chunks, n_taps):
    step = pl.program_id(1)

    @pl.when(step == n_chunks)
    def _():
        o_ref[0:os_ref.shape[0], :] = os_ref[...]

    @pl.when(step < n_chunks)
    def _():
        @pl.when(step % chunks_per_seq == 0)
        def _():
            s_ref[...] = jnp.zeros_like(s_ref)

        raw = ((q_ref, qh_ref, wq_ref), (k_ref, kh_ref, wk_ref), (v_ref, vh_ref, wv_ref))
        _gdn_chunk(raw, z_ref, beta_ref, g_ref, grow_ref, gon_ref, o_ref, s_ref, hb=hb, n_taps=n_taps,
                   first=step % chunks_per_seq == 0)


def _gdn_chunk(raw, z_ref, beta_ref, g_ref, grow_ref, gon_ref, o_ref, s_ref, *, hb, n_taps, first):
    C = CHUNK

    def prepared(kind, h):
        x_ref, halo_ref, w_ref = raw[kind]
        sl = slice(h * HEAD_DIM, (h + 1) * HEAD_DIM)
        halo = jnp.where(first, 0.0, halo_ref[:, sl])
        acc = _conv_blocks(lambda b: halo if b < 0 else x_ref[SUBLANES * b:SUBLANES * (b + 1), sl],
                           lambda d: w_ref[n_taps - 1 - d:n_taps - d, sl], n_taps, C // SUBLANES, HEAD_DIM)
        return jnp.concatenate([_qkv_finish(a, kind) for a in acc], axis=0)

    ri = lax.broadcasted_iota(jnp.int32, (C, C), 0)
    ci = lax.broadcasted_iota(jnp.int32, (C, C), 1)
    causal = ri >= ci
    strict = ri > ci
    gon = gon_ref[...]

    heads = range(hb)
    hs = [slice(h * HEAD_DIM, (h + 1) * HEAD_DIM) for h in heads]
    nt = (((1,), (1,)), ((), ()))
    tn = (((0,), (0,)), ((), ()))
    b16 = lambda xs: [x.astype(BF16) for x in xs]
    mm = lambda xs, ys: [jnp.dot(x, y, preferred_element_type=F32) for x, y in zip(xs, ys)]

    k = [prepared(1, h) for h in heads]
    bcol = [beta_ref[0, :, h:h + 1] for h in heads]
    gcol = [g_ref[0, :, h:h + 1] for h in heads]
    grow = [grow_ref[0, h:h + 1, :] for h in heads]
    decay = [jnp.where(causal, jnp.exp(jnp.where(causal, gcol[h] - grow[h], 0.0)), 0.0) for h in heads]
    eg = [jnp.exp(gcol[h]) for h in heads]
    kb = [k[h] * bcol[h] for h in heads]
    k16 = b16(k)
    kk = [lax.dot_general(x, y, nt, preferred_element_type=F32) for x, y in zip(b16(kb), k16)]
    bm = [jnp.where(strict, kk[h] * decay[h], 0.0) for h in heads]

    half = C // 2
    pr = lax.broadcasted_iota(jnp.int32, (half, C), 0)
    pc = lax.broadcasted_iota(jnp.int32, (half, C), 1) % half
    pblk = lambda s: (pr // s) == (pc // s)
    zero16 = jnp.zeros((half, half), BF16)

    def unpack(ps):
        return [jnp.concatenate([jnp.concatenate([x[:, :half], zero16], axis=1),
                                 jnp.concatenate([zero16, x[:, half:]], axis=1)], axis=0) for x in ps]

    bmp = [jnp.concatenate([x[:half, :half], x[half:, half:]], axis=1) for x in bm]
    in8 = pblk(SUBLANES)
    x0 = [jnp.where(in8, -x, 0.0) for x in bmp]
    tp = [(pr == pc).astype(F32) + x for x in x0]
    x16 = b16(x0)
    p16 = b16(mm(x16, unpack(x16)))
    pf16 = unpack(p16)
    tp = [t + d for t, d in zip(tp, mm(b16(tp), pf16))]
    pf16 = unpack(b16(mm(p16, pf16)))
    tp = [t + d for t, d in zip(tp, mm(b16(tp), pf16))]
    s = SUBLANES
    while s < half:
        m = jnp.logical_and(pblk(2 * s), jnp.logical_not(pblk(s)))
        off16 = unpack(b16([jnp.where(m, x, 0.0) for x in bmp]))
        tp16 = b16(tp)
        tp = [t - d for t, d in zip(tp, mm(b16(mm(tp16, off16)), unpack(tp16)))]
        s *= 2
    ta16 = b16([t[:, :half] for t in tp])
    tb16 = b16([t[:, half:] for t in tp])
    tc16 = b16([-d for d in mm(b16(mm(tb16, b16([x[half:, :half] for x in bm]))), ta16)])
    t16 = [jnp.concatenate([jnp.concatenate([a, zero16], axis=1), jnp.concatenate([c, b], axis=1)], axis=0)
           for a, b, c in zip(ta16, tb16, tc16)]

    q = [prepared(0, h) for h in heads]
    qk = [lax.dot_general(x, y, nt, preferred_element_type=F32) for x, y in zip(b16(q), k16)]
    attn16 = b16([qk[h] * decay[h] for h in heads])
    v = [prepared(2, h) for h in heads]
    sol = mm(t16, [jnp.concatenate([(v[h] * bcol[h]).astype(BF16), (kb[h] * eg[h]).astype(BF16)], axis=1)
                   for h in heads])
    value = [x[:, :HEAD_DIM] for x in sol]
    kcd16 = b16([x[:, HEAD_DIM:] for x in sol])
    st = [s_ref[0, h] for h in heads]
    st16 = b16(st)
    u16 = b16([value[h] - d for h, d in zip(heads, mm(kcd16, st16))])
    o = [a + c for a, c in zip(mm(b16([q[h] * eg[h] for h in heads]), st16), mm(attn16, u16))]
    glast = [gcol[h][C - 1:C, :] for h in heads]
    kd16 = b16([k[h] * jnp.exp(glast[h] - gcol[h]) for h in heads])
    ku = [lax.dot_general(x, y, tn, preferred_element_type=F32) for x, y in zip(kd16, u16)]
    for h in heads:
        s_ref[0, h] = st[h] * jnp.exp(glast[h]) + ku[h]
        on = o[h] * lax.rsqrt(jnp.mean(o[h] * o[h], axis=-1, keepdims=True) + RMS_EPS) * gon
        o_ref[:, hs[h]] = (on * _silu(z_ref[:, hs[h]])).astype(o_ref.dtype)


def _gdn_scan_prompt(proj, w_conv, beta, gcum, g_onorm, o_sample, Bp, T, H):
    Mp = Bp * T
    n_taps = w_conv.shape[0]
    z_col0 = 3 * H * HEAD_DIM
    assert n_taps - 1 <= SUBLANES
    Bs = o_sample.shape[0]
    hb = _pick_tile(H, 8, 8) if H % 8 == 0 else H
    tt = CHUNK
    assert T % CHUNK == 0 and (hb % 8 == 0 or hb == H) and Bs <= tt
    nhg, ntt = H // hb, T // tt
    n_chunks = Bp * ntt
    cw = hb * HEAD_DIM
    assert z_col0 % cw == 0
    beta_g = beta.reshape(Mp, nhg, hb).transpose(1, 0, 2)
    gcol_g = gcum.reshape(Mp, nhg, hb).transpose(1, 0, 2)
    grow_g = gcum.reshape(Mp // CHUNK, CHUNK, H).transpose(0, 2, 1)
    body = functools.partial(_gdn_scan_body, hb=hb, chunks_per_seq=ntt, n_chunks=n_chunks, n_taps=n_taps)
    row = lambda s: jnp.minimum(s, n_chunks - 1)
    before = lambda s: jnp.maximum(row(s) * (tt // SUBLANES) - 1, 0)
    blk = (4 * _nbytes((tt, cw), F32) + 2 * _nbytes((tt, LANES), F32) + _nbytes((hb, LANES), F32)
           + 6 * _nbytes((SUBLANES, cw), F32)
           + _nbytes((tt, cw), BF16) + _nbytes((Bs, cw), BF16) + _nbytes((hb, HEAD_DIM, HEAD_DIM), F32))
    temporaries = 8 * hb * _nbytes((CHUNK, CHUNK), F32)
    return pl.pallas_call(
        body,
        grid=(nhg, n_chunks + 1),
        in_specs=[
            pl.BlockSpec((tt, cw), lambda h, s: (row(s), h)),
            pl.BlockSpec((tt, cw), lambda h, s: (row(s), nhg + h)),
            pl.BlockSpec((tt, cw), lambda h, s: (row(s), 2 * nhg + h)),
            pl.BlockSpec((SUBLANES, cw), lambda h, s: (before(s), h)),
            pl.BlockSpec((SUBLANES, cw), lambda h, s: (before(s), nhg + h)),
            pl.BlockSpec((SUBLANES, cw), lambda h, s: (before(s), 2 * nhg + h)),
            pl.BlockSpec((n_taps, cw), lambda h, s: (0, h)),
            pl.BlockSpec((n_taps, cw), lambda h, s: (0, nhg + h)),
            pl.BlockSpec((n_taps, cw), lambda h, s: (0, 2 * nhg + h)),
            pl.BlockSpec((tt, cw), lambda h, s: (row(s), z_col0 // cw + h)),
            pl.BlockSpec((1, tt, hb), lambda h, s: (h, row(s), 0)),
            pl.BlockSpec((1, tt, hb), lambda h, s: (h, row(s), 0)),
            pl.BlockSpec((1, hb, CHUNK), lambda h, s: (row(s), h, 0)),
            pl.BlockSpec((1, HEAD_DIM), lambda h, s: (0, 0)),
            pl.BlockSpec((Bs, cw), lambda h, s: (0, h)),
        ],
        out_specs=[
            pl.BlockSpec((tt, cw), lambda h, s: (s, h)),
            pl.BlockSpec((1, hb, HEAD_DIM, HEAD_DIM), lambda h, s: (row(s) // ntt, h, 0, 0)),
        ],
        out_shape=[jax.ShapeDtypeStruct((Mp + Bs, H * HEAD_DIM), BF16),
                   jax.ShapeDtypeStruct((Bp, H, HEAD_DIM, HEAD_DIM), F32)],
        compiler_params=_params(("parallel", "arbitrary"), blk, temporaries),
        name="gdn_scan_prompt",
    )(proj, proj, proj, proj, proj, proj, w_conv, w_conv, w_conv, proj, beta_g, gcol_g, grow_g,
      g_onorm.reshape(1, HEAD_DIM), o_sample)


def _gdn_step_body(a_ref, beta_ref, s_ref, kt_ref, qt_ref, v_ref, z_ref, gon_ref, so_ref, o_ref, orow_ref, *,
                   n_heads, seqs):
    def one_sequence(i, carry):
        b = pl.program_id(0) * seqs + i
        for h in range(n_heads):
            st = s_ref[i, h]
            a = a_ref[b, h]
            bt = beta_ref[b, h]
            kc = jnp.broadcast_to(kt_ref[i, :, h:h + 1], st.shape)
            qc = jnp.broadcast_to(qt_ref[i, :, h:h + 1], st.shape)
            sk = jnp.sum(st * kc, axis=0, keepdims=True)
            w = v_ref[i, h:h + 1, :] - a * sk
            sn = a * st + (bt * kc) * w
            so_ref[i, h] = sn
            orow_ref[h:h + 1, :] = jnp.sum(sn * qc, axis=0, keepdims=True)
        o = orow_ref[...]
        on = o * lax.rsqrt(jnp.mean(o * o, axis=-1, keepdims=True) + RMS_EPS) * gon_ref[...]
        o_ref[i] = (on * _silu(z_ref[i])).astype(o_ref.dtype)
        return carry

    lax.fori_loop(0, seqs, one_sequence, 0)


def _gdn_step_sample(qkv_s, z_s, beta_s, a_s, state, g_onorm):
    Bs, H = beta_s.shape
    hd = H * HEAD_DIM
    q3 = qkv_s[:, :hd].reshape(Bs, H, HEAD_DIM).transpose(0, 2, 1)
    k3 = qkv_s[:, hd:2 * hd].reshape(Bs, H, HEAD_DIM).transpose(0, 2, 1)
    v3 = qkv_s[:, 2 * hd:].reshape(Bs, H, HEAD_DIM)
    z3 = z_s.reshape(Bs, H, HEAD_DIM)
    smem = pl.BlockSpec(memory_space=pltpu.SMEM)
    seqs = _pick_tile(Bs, 4, 1)
    body = functools.partial(_gdn_step_body, n_heads=H, seqs=seqs)
    blk = seqs * (2 * _nbytes((H, HEAD_DIM, HEAD_DIM), F32) + 6 * _nbytes((HEAD_DIM, LANES), F32))
    so, o = pl.pallas_call(
        body,
        grid=(Bs // seqs,),
        in_specs=[smem, smem,
                  pl.BlockSpec((seqs, H, HEAD_DIM, HEAD_DIM), lambda b: (b, 0, 0, 0)),
                  pl.BlockSpec((seqs, HEAD_DIM, H), lambda b: (b, 0, 0)),
                  pl.BlockSpec((seqs, HEAD_DIM, H), lambda b: (b, 0, 0)),
                  pl.BlockSpec((seqs, H, HEAD_DIM), lambda b: (b, 0, 0)),
                  pl.BlockSpec((seqs, H, HEAD_DIM), lambda b: (b, 0, 0)),
                  pl.BlockSpec((1, HEAD_DIM), lambda b: (0, 0))],
        out_specs=[pl.BlockSpec((seqs, H, HEAD_DIM, HEAD_DIM), lambda b: (b, 0, 0, 0)),
                   pl.BlockSpec((seqs, H, HEAD_DIM), lambda b: (b, 0, 0))],
        out_shape=[jax.ShapeDtypeStruct((Bs, H, HEAD_DIM, HEAD_DIM), F32),
                   jax.ShapeDtypeStruct((Bs, H, HEAD_DIM), BF16)],
        scratch_shapes=[pltpu.VMEM((H, HEAD_DIM), F32)],
        compiler_params=_params(("arbitrary",), blk),
        name="gdn_step_sample",
    )(a_s, beta_s, state, k3, q3, v3, z3, g_onorm.reshape(1, HEAD_DIM))
    return o.reshape(Bs, hd), so


def _seq_tails(a, Bp, T, n, ncols):
    return jnp.stack([lax.slice(a, (b * T + T - n, 0), ((b + 1) * T, ncols)) for b in range(Bp)])


def _wcol_bytes(*ws):
    return sum(w.shape[1] * w.dtype.itemsize for w in ws)


def _conformer_layer(h, xn, g_next, Mp, Bp, T, Bs, state, j, w_pw1, w_dw, b_dw, ln_g, ln_b, w_pw2):
    M, D = h.shape
    Dc = w_pw1.shape[2] // 2
    tm, tn = _mm_tiles(M, D, Dc, 2 * _wcol_bytes(w_pw1))
    glu = _matmul(_mm_glu_body, "mm_glu", M, Dc, tm, tn,
                  _xn_specs(xn, tm) + [_w_spec(w_pw1, j, tn), _w_spec(w_pw1, j, tn, Dc // tn)], F32)
    c_s, new_s = _conf_conv_sample(glu, Mp, Bs, jnp.swapaxes(state, 0, 1), w_dw, b_dw, ln_g, ln_b)
    new_s = jnp.swapaxes(new_s, 0, 1)
    c = _conf_conv_prompt(glu, Bp, T, c_s, w_dw, b_dw, ln_g, ln_b)
    tm, tn = _mm_tiles(M, Dc, D, _wcol_bytes(w_pw2))
    h, xn = _matmul(_mm_res_body, "mm_pw2", M, D, tm, tn,
                    [_x_spec(c, tm), _w_spec(w_pw2, j, tn), _r_spec(h, tm, tn), _g_spec(g_next, tn)], F32, True)
    new_p = _seq_tails(glu, Bp, T, w_dw.shape[0] - 1, Dc)
    return h, xn, new_p, new_s


def _gdn_layer(h, xn, g_next, Mp, Bp, T, Bs, conv_state, gdn_state, j, w_in, w_conv, a_log, dt_bias, g_onorm, w_out):
    M, D = h.shape
    H = a_log.shape[0]
    hd = H * HEAD_DIM
    qkv_dim = 3 * hd
    n_main = qkv_dim + hd
    tm, tn = _mm_tiles(M, D, n_main, _wcol_bytes(w_in))
    w_in_t = jnp.swapaxes(w_in, 1, 2)
    wt_spec = lambda rows, row_block0: (w_in_t, (None, rows, D), lambda i, c: (j, c + row_block0, 0))
    proj = _matmul(_mm_plain_wt_body, "mm_gdn_in", M, n_main, tm, tn, _xn_specs(xn, tm) + [wt_spec(tn, 0)], F32)
    assert n_main % (2 * H) == 0
    logits = _matmul(_mm_plain_wt_body, "mm_gdn_gates", M, 2 * H, tm, 2 * H,
                     _xn_specs(xn, tm) + [wt_spec(2 * H, n_main // (2 * H))], F32)

    beta_p, gcum_p = _gates(logits, a_log, dt_bias, 0, Mp, CHUNK)
    beta_s, a_s = _gates(logits, a_log, dt_bias, Mp, Bs, 1)

    qkv_s, new_conv_s = _qkv_prep_sample(proj, Mp, Bs, qkv_dim, jnp.swapaxes(conv_state, 0, 1), w_conv)
    new_conv_s = jnp.swapaxes(new_conv_s, 0, 1)
    o_s, s_s = _gdn_step_sample(qkv_s, lax.slice(proj, (Mp, qkv_dim), (M, n_main)), beta_s, a_s, gdn_state, g_onorm)

    o, s_p = _gdn_scan_prompt(proj, w_conv, beta_p, gcum_p, g_onorm, o_s, Bp, T, H)
    tm, tn = _mm_tiles(M, hd, D, _wcol_bytes(w_out))
    h, xn = _matmul(_mm_res_body, "mm_gdn_out", M, D, tm, tn,
                    [_x_spec(o, tm), _w_spec(w_out, j, tn), _r_spec(h, tm, tn), _g_spec(g_next, tn)], F32, True)
    new_conv_p = _seq_tails(proj, Bp, T, w_conv.shape[0] - 1, qkv_dim)
    return h, xn, new_conv_p, new_conv_s, s_p, s_s


def _ffn_ple(h, xn, p_all, i, w_gate, w_up, w_down, g_ple, w_pg, w_pp, g_next):
    M, D = h.shape
    F = w_gate.shape[2]
    tm, tn = _mm_tiles(M, D, F, _wcol_bytes(w_gate, w_up))
    up_operands = _xn_specs(xn, tm) + [_w_spec(w_gate, i, tn), _w_spec(w_up, i, tn)]
    n_steps = (M // tm) * (F // tn)
    slab = F // n_steps
    if F % n_steps == 0 and slab % 16 == 0:
        nj = F // tn
        act, w_down16 = _matmul(
            _mm_swiglu_cast_body, "mm_ffn_up", M, F, tm, tn,
            up_operands + [(w_down, (None, slab, D), lambda r, c: (i, r * nj + c, 0))], BF16,
            side_out=((F, D), BF16, (slab, D), lambda r, c: (r * nj + c, 0)))
    else:
        act = _matmul(_mm_swiglu_body, "mm_ffn_up", M, F, tm, tn, up_operands, BF16)
        w_down16 = w_down[i].astype(BF16)
    w_down16 = w_down16[None]
    tm, tn = _mm_tiles(M, F, D, _wcol_bytes(w_down16))
    h, xn = _matmul(_mm_res_body, "mm_ffn_down", M, D, tm, tn,
                    [_x_spec(act, tm), _w_spec(w_down16, 0, tn), _r_spec(h, tm, tn), _g_spec(g_ple, tn)], F32, True)
    tm, tn = _mm_tiles(M, D, D, _wcol_bytes(w_pg, w_pp))
    p_spec = (p_all, (None, tm, p_all.shape[2]), lambda r, c: (i, r, 0))
    operands = _xn_specs(xn, tm) + [p_spec, _w_spec(w_pg, i, tn), _w_spec(w_pp, i, tn), _r_spec(h, tm, tn)]
    if g_next is None:
        return _matmul(_mm_ple_last_body, "mm_ple", M, D, tm, tn, operands, F32), None
    return _matmul(_mm_ple_body, "mm_ple", M, D, tm, tn, operands + [_g_spec(g_next, tn)], F32, True)


def kernel(x_prompt, x_sample, p_prompt, p_sample, state_conv_conformer, state_conv_qkv, state_gdn, g_mix, g_ffn, g_ple, g_final, conf_w_pw1, conf_w_dw, conf_b_dw, conf_ln_g, conf_ln_b, conf_w_pw2, gdn_w_in, gdn_w_conv, gdn_a_log, gdn_dt_bias, gdn_g_onorm, gdn_w_out, ffn_w_gate, ffn_w_up, ffn_w_down, ple_w_gate, ple_w_proj):
    Bp, T, D = x_prompt.shape
    Bs, Ts, _ = x_sample.shape
    assert Ts == 1
    depth = g_mix.shape[0]
    Mp = Bp * T
    M = Mp + Bs
    h, xn = _embed(x_prompt.reshape(Mp, D), x_sample.reshape(Bs, D), g_mix[0])
    p_all = jnp.concatenate([p_prompt.reshape(depth, Mp, -1), p_sample.reshape(depth, Bs, -1)], axis=1).astype(BF16)
    conf_p, conf_s, qkv_p, qkv_s, gdn_p, gdn_s = [], [], [], [], [], []
    for i in range(depth):
        j = i // 2
        if i % 2 == 0:
            h, xn, np_, ns_ = _conformer_layer(h, xn, g_ffn[i], Mp, Bp, T, Bs, state_conv_conformer[j], j, conf_w_pw1,
                                               conf_w_dw[j], conf_b_dw[j], conf_ln_g[j], conf_ln_b[j], conf_w_pw2)
            conf_p.append(np_)
            conf_s.append(ns_)
        else:
            h, xn, cp_, cs_, sp_, ss_ = _gdn_layer(h, xn, g_ffn[i], Mp, Bp, T, Bs, state_conv_qkv[j], state_gdn[j], j,
                                                   gdn_w_in, gdn_w_conv[j], gdn_a_log[j], gdn_dt_bias[j],
                                                   gdn_g_onorm[j], gdn_w_out)
            qkv_p.append(cp_)
            qkv_s.append(cs_)
            gdn_p.append(sp_)
            gdn_s.append(ss_)
        h, xn = _ffn_ple(h, xn, p_all, i, ffn_w_gate, ffn_w_up, ffn_w_down, g_ple[i], ple_w_gate, ple_w_proj,
                         g_mix[i + 1] if i + 1 < depth else None)

    y_p = _rmsnorm(h, g_final, F32, 0, Mp).reshape(Bp, T, D)
    y_s = _rmsnorm(h, g_final, F32, Mp, Bs).reshape(Bs, 1, D)
    return (y_p, y_s, jnp.stack(conf_p), jnp.stack(qkv_p), jnp.stack(gdn_p),
            jnp.stack(conf_s), jnp.stack(qkv_s), jnp.stack(gdn_s))
```
